```python
import jax, jax.numpy as jnp
from jax import lax
import numpy as np

D_MODEL = 1024
BATCH = 16
SEQ = 2048
DEPTH = 2

GRID_W = 64
CTX_LEN = 256
N_HEADS = 8
N_KV_HEADS = 2
HEAD_DIM = 64
GROUP = N_HEADS // N_KV_HEADS
ATTN_W = N_HEADS * HEAD_DIM
KV_W = N_KV_HEADS * HEAD_DIM
SHORT_W = 512
SHORT_K = 3
CONF_W = 512
CONF_K = 31
N_BRANCH = 3
Q_BLOCK = 128
ROPE_THETA = 10000.0
RMS_EPS = 1e-6
LN_EPS = 1e-5
SPLIT_SIZES = (ATTN_W, KV_W, KV_W, ATTN_W,
               SHORT_W, SHORT_W, SHORT_W, SHORT_W,
               CONF_W, CONF_W, CONF_W,
               N_BRANCH * D_MODEL)
D_IN = sum(SPLIT_SIZES)
K_LO = ATTN_W
V_HI = ATTN_W + 2 * KV_W

kernel_name = "hybrid_gqa_shortconv_conformer_dit_block"


def _split_proj(p):
    idx = [int(i) for i in np.cumsum(SPLIT_SIZES)[:-1]]
    return jnp.split(p, idx, axis=-1)


def _layer_norm(x, g, b):
    xf = x.astype(jnp.float32)
    mu = jnp.mean(xf, axis=-1, keepdims=True)
    var = jnp.mean(jnp.square(xf - mu), axis=-1, keepdims=True)
    return ((xf - mu) * lax.rsqrt(var + LN_EPS)).astype(x.dtype) * g + b


def _rms_norm(x, g):
    xf = x.astype(jnp.float32)
    return (xf * lax.rsqrt(jnp.mean(jnp.square(xf), axis=-1, keepdims=True) + RMS_EPS)).astype(x.dtype) * g


def _depthwise_conv(x, w):
    k, ch = w.shape
    return lax.conv_general_dilated(x, w[:, None, :], window_strides=(1,), padding=[(k // 2, k // 2)],
                                    dimension_numbers=('NWC', 'WIO', 'NWC'), feature_group_count=ch)


def _axial_rope_tables(n_tokens):
    n_rows = n_tokens // GRID_W
    rows = jnp.repeat(jnp.arange(n_rows, dtype=jnp.float32), GRID_W)
    cols = jnp.tile(jnp.arange(GRID_W, dtype=jnp.float32), n_rows)
    quarter = HEAD_DIM // 4
    freqs = ROPE_THETA ** (-jnp.arange(quarter, dtype=jnp.float32) / quarter)
    ang_r = rows[:, None] * freqs
    ang_c = cols[:, None] * freqs
    return (jnp.cos(ang_r)[:, None, :], jnp.sin(ang_r)[:, None, :],
            jnp.cos(ang_c)[:, None, :], jnp.sin(ang_c)[:, None, :])


def _rot_half(xh, cos, sin):
    q = xh.shape[-1] // 2
    x1, x2 = xh[..., :q], xh[..., q:]
    return jnp.concatenate([x1 * cos - x2 * sin, x2 * cos + x1 * sin], axis=-1)


def _apply_axial_rope(x, tabs):
    cr, sr, cc, sc = tabs
    xf = x.astype(jnp.float32)
    half = HEAD_DIM // 2
    out = jnp.concatenate([_rot_half(xf[..., :half], cr, sr), _rot_half(xf[..., half:], cc, sc)], axis=-1)
    return out.astype(x.dtype)


def _attend(q, k, v):
    s = jnp.einsum('bqkgd,bskd->bkgqs', q, k).astype(jnp.float32) * (HEAD_DIM ** -0.5)
    p = jax.nn.softmax(s, axis=-1).astype(v.dtype)
    return jnp.einsum('bkgqs,bskd->bqkgd', p, v)


def _blocked_attention(q, k, v):
    b, s = q.shape[0], q.shape[1]
    nb = s // Q_BLOCK
    qb = jnp.moveaxis(q.reshape(b, nb, Q_BLOCK, N_KV_HEADS, GROUP, HEAD_DIM), 1, 0)
    o = lax.map(lambda blk: _attend(blk, k, v), qb)
    return jnp.moveaxis(o, 0, 1).reshape(b, s, ATTN_W)


def _heads(t, n):
    return t.reshape(t.shape[0], t.shape[1], n, HEAD_DIM)


def _merge_branches(o_attn, parts, w_pa, w_sc, w_ps, w_cdw, b_cdw, cln_g, cln_b, w_pc, b_pc, w_o):
    _, _, _, z_a, s_b, s_c, s_h, z_s, c_a, c_g, z_c, g_m = parts
    br_a = (o_attn * jax.nn.silu(z_a)) @ w_pa
    br_b = (s_b * _depthwise_conv(s_c * s_h, w_sc) * jax.nn.silu(z_s)) @ w_ps
    hc = c_a * jax.nn.sigmoid(c_g)
    hc = _depthwise_conv(hc, w_cdw) + b_cdw
    hc = jax.nn.silu(_layer_norm(hc, cln_g, cln_b))
    br_c = (hc * jax.nn.silu(z_c)) @ w_pc + b_pc
    g_a, g_b, g_c = jnp.split(g_m, N_BRANCH, axis=-1)
    merged = jax.nn.sigmoid(g_a) * br_a + jax.nn.sigmoid(g_b) * br_b + jax.nn.sigmoid(g_c) * br_c
    return merged @ w_o


def setup_inputs(seed: int = 0) -> dict:
    key = jax.random.key(seed)
    ks = jax.random.split(key, 24)
    f32 = jnp.float32
    beta = (8.0 * DEPTH) ** -0.25

    def nrm(k, shape, scale):
        return jax.random.normal(k, shape, f32) * scale

    return {
        "x": nrm(ks[0], (BATCH, SEQ, D_MODEL), 1.0),
        "c": nrm(ks[1], (BATCH, D_MODEL), 1.0),
        "ctx": nrm(ks[2], (BATCH, CTX_LEN, D_MODEL), 1.0),
        "c_ctx": nrm(ks[3], (D_MODEL,), 1.0),
        "w_mod": nrm(ks[4], (DEPTH, D_MODEL, 3 * D_MODEL), D_MODEL ** -0.5),
        "b_mod": nrm(ks[5], (DEPTH, 3 * D_MODEL), 0.02),
        "w_in": nrm(ks[6], (DEPTH, D_MODEL, D_IN), D_MODEL ** -0.5),
        "q_gain": 1.0 + nrm(ks[7], (DEPTH, HEAD_DIM), 0.02),
        "k_gain": 1.0 + nrm(ks[8], (DEPTH, HEAD_DIM), 0.02),
        "w_proj_attn": nrm(ks[9], (DEPTH, ATTN_W, D_MODEL), beta * ATTN_W ** -0.5),
        "w_short_conv": nrm(ks[10], (DEPTH, SHORT_K, SHORT_W), SHORT_K ** -0.5),
        "w_proj_short": nrm(ks[11], (DEPTH, SHORT_W, D_MODEL), beta * SHORT_W ** -0.5),
        "w_conf_dw": nrm(ks[12], (DEPTH, CONF_K, CONF_W), CONF_K ** -0.5),
        "b_conf_dw": nrm(ks[13], (DEPTH, CONF_W), 0.02),
        "conf_ln_g": 1.0 + nrm(ks[14], (DEPTH, CONF_W), 0.02),
        "conf_ln_b": nrm(ks[15], (DEPTH, CONF_W), 0.02),
        "w_proj_conf": nrm(ks[16], (DEPTH, CONF_W, D_MODEL), beta * CONF_W ** -0.5),
        "b_proj_conf": nrm(ks[17], (DEPTH, D_MODEL), 0.02),
        "w_out": nrm(ks[18], (DEPTH, D_MODEL, D_MODEL), beta * D_MODEL ** -0.5),
        "post_ln_g": 1.0 + nrm(ks[19], (DEPTH, D_MODEL), 0.02),
        "post_ln_b": nrm(ks[20], (DEPTH, D_MODEL), 0.02),
    }


def reference(x, c, ctx, c_ctx, w_mod, b_mod, w_in, q_gain, k_gain, w_proj_attn, w_short_conv,
              w_proj_short, w_conf_dw, b_conf_dw, conf_ln_g, conf_ln_b, w_proj_conf, b_proj_conf,
              w_out, post_ln_g, post_ln_b):
    alpha = (2.0 * DEPTH) ** 0.25
    b, s = x.shape[0], x.shape[1]
    lc = ctx.shape[1]
    tabs = _axial_rope_tables(s)
    h_lat, h_ctx = x, ctx
    for l in range(DEPTH):
        last = l == DEPTH - 1
        branch_w = (w_proj_attn[l], w_short_conv[l], w_proj_short[l], w_conf_dw[l], b_conf_dw[l],
                    conf_ln_g[l], conf_ln_b[l], w_proj_conf[l], b_proj_conf[l], w_out[l])
        shift, scale, gate = jnp.split(jax.nn.silu(c) @ w_mod[l] + b_mod[l], 3, axis=-1)
        shift_c, scale_c, gate_c = jnp.split(jax.nn.silu(c_ctx) @ w_mod[l] + b_mod[l], 3, axis=-1)
        u_lat = h_lat * (1.0 + scale[:, None, :]) + shift[:, None, :]
        u_ctx = h_ctx * (1.0 + scale_c) + shift_c

        if last:
            k_c, v_c = jnp.split(u_ctx @ w_in[l][:, K_LO:V_HI], 2, axis=-1)
        else:
            p_ctx = _split_proj(u_ctx @ w_in[l])
            k_c, v_c = p_ctx[1], p_ctx[2]
        k_c = _rms_norm(_heads(k_c, N_KV_HEADS), k_gain[l])
        v_c = _heads(v_c, N_KV_HEADS)

        p_lat = _split_proj(u_lat @ w_in[l])
        q_l = _apply_axial_rope(_rms_norm(_heads(p_lat[0], N_HEADS), q_gain[l]), tabs)
        q_l = q_l.reshape(b, s, N_KV_HEADS, GROUP, HEAD_DIM)
        k_l = _apply_axial_rope(_rms_norm(_heads(p_lat[1], N_KV_HEADS), k_gain[l]), tabs)
        v_l = _heads(p_lat[2], N_KV_HEADS)
        k_all = jnp.concatenate([k_c, k_l], axis=1)
        v_all = jnp.concatenate([v_c, v_l], axis=1)
        o_lat = _blocked_attention(q_l, k_all, v_all)
        out_lat = _merge_branches(o_lat, p_lat, *branch_w)

        if not last:
            q_c = _rms_norm(_heads(p_ctx[0], N_HEADS), q_gain[l]).reshape(b, lc, N_KV_HEADS, GROUP, HEAD_DIM)
            o_ctx = _attend(q_c, k_c, v_c).reshape(b, lc, ATTN_W)
            out_ctx = _merge_branches(o_ctx, p_ctx, *branch_w)
            h_ctx = _layer_norm(alpha * h_ctx + gate_c * out_ctx, post_ln_g[l], post_ln_b[l])

        h_lat = _layer_norm(alpha * h_lat + gate[:, None, :] * out_lat, post_ln_g[l], post_ln_b[l])
    return h_lat
```

```python
import functools

import jax
import jax.numpy as jnp
import numpy as np
from jax import lax
from jax.experimental import pallas as pl
from jax.experimental.pallas import tpu as pltpu

D_MODEL = 1024
DEPTH = 2
GRID_W = 64
N_HEADS = 8
N_KV_HEADS = 2
HEAD_DIM = 64
GROUP = N_HEADS // N_KV_HEADS
ATTN_W = N_HEADS * HEAD_DIM
KV_W = N_KV_HEADS * HEAD_DIM
SHORT_W = 512
SHORT_K = 3
CONF_W = 512
CONF_K = 31
N_BRANCH = 3
ROPE_THETA = 10000.0
RMS_EPS = 1e-6
LN_EPS = 1e-5
D_IN = 2 * ATTN_W + 2 * KV_W + 4 * SHORT_W + 3 * CONF_W + N_BRANCH * D_MODEL

_OFF = {}
_o = 0
for _name, _w in (("q", ATTN_W), ("k", KV_W), ("v", KV_W), ("z_a", ATTN_W), ("s_b", SHORT_W),
                  ("s_c", SHORT_W), ("s_h", SHORT_W), ("z_s", SHORT_W), ("c_a", CONF_W),
                  ("c_g", CONF_W), ("z_c", CONF_W), ("g_m", N_BRANCH * D_MODEL)):
    _OFF[_name] = (_o, _o + _w)
    _o += _w

LANES = 128
HALO = 16
MOD_ROWS = 24
VMEM_LIMIT_PROJECT = 56 * 1024 * 1024
VMEM_LIMIT_DEFAULT = 48 * 1024 * 1024

F32 = jnp.float32
BF16 = jnp.bfloat16


def _sigmoid(x):
    return 1.0 / (1.0 + jnp.exp(-x))


def _silu(x):
    return x * _sigmoid(x)


def _mod_kernel(c_ref, w_ref, b_ref, o_ref):
    a = _silu(c_ref[...])
    o_ref[0] = jnp.dot(a, w_ref[0], precision=lax.Precision.HIGHEST,
                       preferred_element_type=F32) + b_ref[0]


def _modulation(c_all, w_mod, b_mod):
    n_col = 3 * D_MODEL // D_MODEL
    return pl.pallas_call(
        _mod_kernel,
        grid=(DEPTH, n_col),
        in_specs=[
            pl.BlockSpec((MOD_ROWS, D_MODEL), lambda l, j: (0, 0)),
            pl.BlockSpec((1, D_MODEL, D_MODEL), lambda l, j: (l, 0, j)),
            pl.BlockSpec((1, 1, D_MODEL), lambda l, j: (l, 0, j)),
        ],
        out_specs=pl.BlockSpec((1, MOD_ROWS, D_MODEL), lambda l, j: (l, 0, j)),
        out_shape=jax.ShapeDtypeStruct((DEPTH, MOD_ROWS, 3 * D_MODEL), F32),
        compiler_params=pltpu.CompilerParams(
            dimension_semantics=("arbitrary", "arbitrary"), vmem_limit_bytes=VMEM_LIMIT_DEFAULT),
        name="modulation",
    )(c_all, w_mod, b_mod.reshape(DEPTH, 1, 3 * D_MODEL))


def _norm_rope(x, gain, cos, sin):
    lane = lax.broadcasted_iota(jnp.int32, x.shape, 1)
    first = lane < HEAD_DIM
    sq = x * x
    s_lo = jnp.sum(jnp.where(first, sq, 0.0), axis=-1, keepdims=True)
    s_hi = jnp.sum(jnp.where(first, 0.0, sq), axis=-1, keepdims=True)
    ms = jnp.where(first, s_lo, s_hi) * (1.0 / HEAD_DIM)
    xn = x * lax.rsqrt(ms + RMS_EPS) * gain
    quarter = HEAD_DIM // 4
    up = pltpu.roll(xn, LANES - quarter, 1)
    down = pltpu.roll(xn, quarter, 1)
    partner = jnp.where((lane & (2 * quarter - 1)) < quarter, up, down)
    return xn * cos + partner * sin


def _modulate(h_ref, mod_ref):
    shift = mod_ref[0, :, 0:D_MODEL]
    scale = mod_ref[0, :, D_MODEL:2 * D_MODEL]
    return (h_ref[0] * (1.0 + scale) + shift).astype(BF16)


def _project_kernel(h_ref, mod_ref, w_ref, tab_ref, qg_ref, kg_ref,
                    q_ref, k_ref, v_ref, za_ref, yc_ref, bzc_ref, gm_ref):
    u = _modulate(h_ref, mod_ref)
    cos = tab_ref[:, 0:LANES]
    sin = tab_ref[:, LANES:2 * LANES]

    def proj(name, sub=None):
        lo, hi = _OFF[name]
        if sub is not None:
            lo, hi = lo + sub[0], lo + sub[1]
        return jnp.dot(u, w_ref[:, lo:hi], preferred_element_type=F32)

    for j in range(ATTN_W // LANES):
        x = proj("q", (j * LANES, (j + 1) * LANES))
        q = _norm_rope(x, qg_ref[...], cos, sin) * (HEAD_DIM ** -0.5)
        q_ref[0, :, j * LANES:(j + 1) * LANES] = q.astype(BF16)
    k_ref[0] = _norm_rope(proj("k"), kg_ref[...], cos, sin).astype(BF16)
    v_ref[0] = proj("v").astype(BF16)
    za_ref[0] = _silu(proj("z_a")).astype(BF16)
    yc_ref[0, :, 0:SHORT_W] = (proj("s_c") * proj("s_h")).astype(BF16)
    yc_ref[0, :, SHORT_W:SHORT_W + CONF_W] = (proj("c_a") * _sigmoid(proj("c_g"))).astype(BF16)
    bzc_ref[0, :, 0:SHORT_W] = (proj("s_b") * _silu(proj("z_s"))).astype(BF16)
    bzc_ref[0, :, SHORT_W:SHORT_W + CONF_W] = _silu(proj("z_c")).astype(BF16)
    for j in range(N_BRANCH):
        sub = (j * D_MODEL, (j + 1) * D_MODEL)
        gm_ref[0, :, sub[0]:sub[1]] = _sigmoid(proj("g_m", sub)).astype(BF16)


def _project_kv_kernel(h_ref, mod_ref, w_ref, tab_ref, kg_ref, k_ref, v_ref):
    u = _modulate(h_ref, mod_ref)
    cos = tab_ref[:, 0:LANES]
    sin = tab_ref[:, LANES:2 * LANES]
    x = jnp.dot(u, w_ref[...], preferred_element_type=F32)
    k_ref[0] = _norm_rope(x[:, 0:KV_W], kg_ref[...], cos, sin).astype(BF16)
    v_ref[0] = x[:, KV_W:2 * KV_W].astype(BF16)


def _project(h, mod, w_bf, tabs, q_gain, k_gain, tm):
    n_batch, seq, _ = h.shape
    tok = lambda cols: pl.BlockSpec((1, tm, cols), lambda b, i: (b, i, 0))
    out_cols = (ATTN_W, KV_W, KV_W, ATTN_W, SHORT_W + CONF_W, SHORT_W + CONF_W, N_BRANCH * D_MODEL)
    return pl.pallas_call(
        _project_kernel,
        grid=(n_batch, seq // tm),
        in_specs=[
            tok(D_MODEL),
            pl.BlockSpec((1, 1, 3 * D_MODEL), lambda b, i: (b, 0, 0)),
            pl.BlockSpec((D_MODEL, D_IN), lambda b, i: (0, 0), pipeline_mode=pl.Buffered(1)),
            pl.BlockSpec((tm, 2 * LANES), lambda b, i: (i, 0)),
            pl.BlockSpec((1, LANES), lambda b, i: (0, 0)),
            pl.BlockSpec((1, LANES), lambda b, i: (0, 0)),
        ],
        out_specs=[tok(c) for c in out_cols],
        out_shape=[jax.ShapeDtypeStruct((n_batch, seq, c), BF16) for c in out_cols],
        compiler_params=pltpu.CompilerParams(
            dimension_semantics=("arbitrary", "arbitrary"), vmem_limit_bytes=VMEM_LIMIT_PROJECT),
        name="project",
    )(h, mod, w_bf, tabs, q_gain, k_gain)


def _project_kv(h, mod, w_kv_bf, tabs, k_gain, tm):
    n_batch, seq, _ = h.shape
    tok = lambda cols: pl.BlockSpec((1, tm, cols), lambda b, i: (b, i, 0))
    return pl.pallas_call(
        _project_kv_kernel,
        grid=(n_batch, seq // tm),
        in_specs=[
            tok(D_MODEL),
            pl.BlockSpec((1, 1, 3 * D_MODEL), lambda b, i: (b, 0, 0)),
            pl.BlockSpec((D_MODEL, 2 * KV_W), lambda b, i: (0, 0)),
            pl.BlockSpec((tm, 2 * LANES), lambda b, i: (i, 0)),
            pl.BlockSpec((1, LANES), lambda b, i: (0, 0)),
        ],
        out_specs=[tok(KV_W), tok(KV_W)],
        out_shape=[jax.ShapeDtypeStruct((n_batch, seq, KV_W), BF16)] * 2,
        compiler_params=pltpu.CompilerParams(
            dimension_semantics=("arbitrary", "arbitrary"), vmem_limit_bytes=VMEM_LIMIT_DEFAULT),
        name="project_kv",
    )(h, mod, w_kv_bf, tabs, k_gain)


def _attend_kernel(q_ref, k_ref, v_ref, za_ref, o_ref):
    for j in range(ATTN_W // LANES):
        kv = (2 * j) // GROUP
        qp = q_ref[0, :, j * LANES:(j + 1) * LANES]
        acc = None
        for half in range(2):
            s = lax.dot_general(qp, k_ref[0, 2 * kv + half], (((1,), (1,)), ((), ())),
                                preferred_element_type=F32)
            m = jnp.max(s, axis=-1, keepdims=True)
            p = jnp.exp(s - m)
            l = jnp.sum(p, axis=-1, keepdims=True)
            o = jnp.dot(p.astype(BF16), v_ref[0, 2 * kv + half], preferred_element_type=F32)
            o = o * (1.0 / l)
            acc = o if acc is None else acc + o
        za = za_ref[0, :, j * LANES:(j + 1) * LANES].astype(F32)
        o_ref[0, :, j * LANES:(j + 1) * LANES] = (acc * za).astype(BF16)


def _attend(q, k4, v4, za, tq):
    n_batch, seq, _ = q.shape
    n_keys = k4.shape[2]
    tok = pl.BlockSpec((1, tq, ATTN_W), lambda b, i: (b, i, 0))
    keys = pl.BlockSpec((1, 4, n_keys, LANES), lambda b, i: (b, 0, 0, 0))
    return pl.pallas_call(
        _attend_kernel,
        grid=(n_batch, seq // tq),
        in_specs=[tok, keys, keys, tok],
        out_specs=tok,
        out_shape=jax.ShapeDtypeStruct((n_batch, seq, ATTN_W), BF16),
        compiler_params=pltpu.CompilerParams(
            dimension_semantics=("arbitrary", "arbitrary"), vmem_limit_bytes=VMEM_LIMIT_DEFAULT),
        name="attend",
    )(q, k4, v4, za)


def _lane_half_slots(t):
    zero = jnp.zeros_like(t[..., :HEAD_DIM])
    slots = []
    for kv in range(N_KV_HEADS):
        head = t[..., kv * HEAD_DIM:(kv + 1) * HEAD_DIM]
        slots.append(jnp.concatenate([head, zero], axis=-1))
        slots.append(jnp.concatenate([zero, head], axis=-1))
    return jnp.stack(slots, axis=1)


def _layer_norm(x, g, b):
    mu = jnp.mean(x, axis=-1, keepdims=True)
    xc = x - mu
    var = jnp.mean(xc * xc, axis=-1, keepdims=True)
    return xc * lax.rsqrt(var + LN_EPS) * g + b


def _merge_kernel(tile, row_chunk, alpha,
                  ga_ref, yc_ref, ycp_ref, ycn_ref, bzc_ref, gm_ref, h_ref, mod_ref,
                  wpa_ref, wps_ref, wpc_ref, wo_ref, wsc_ref, wcdw_ref, bcdw_ref,
                  clng_ref, clnb_ref, bpc_ref, plg_ref, plb_ref,
                  o_ref, xbuf, act_b, act_c):
    i = pl.program_id(1)
    n_tiles = pl.num_programs(1)

    xbuf[0:HALO] = ycp_ref[0].astype(F32)
    xbuf[HALO:HALO + tile] = yc_ref[0].astype(F32)
    xbuf[HALO + tile:HALO + tile + HALO] = ycn_ref[0].astype(F32)

    @pl.when(i == 0)
    def _():
        xbuf[0:HALO] = jnp.zeros((HALO, SHORT_W + CONF_W), F32)

    @pl.when(i == n_tiles - 1)
    def _():
        xbuf[HALO + tile:HALO + tile + HALO] = jnp.zeros((HALO, SHORT_W + CONF_W), F32)

    for r0 in range(0, tile, row_chunk):
        acc = None
        for k in range(SHORT_K):
            start = HALO + r0 + k - SHORT_K // 2
            term = xbuf[start:start + row_chunk, 0:SHORT_W] * wsc_ref[k:k + 1, :]
            acc = term if acc is None else acc + term
        bz = bzc_ref[0, r0:r0 + row_chunk, 0:SHORT_W].astype(F32)
        act_b[r0:r0 + row_chunk] = (acc * bz).astype(BF16)

        acc = None
        for k in range(CONF_K):
            start = HALO + r0 + k - CONF_K // 2
            term = xbuf[start:start + row_chunk, SHORT_W:SHORT_W + CONF_W] * wcdw_ref[k:k + 1, :]
            acc = term if acc is None else acc + term
        hc = _silu(_layer_norm(acc + bcdw_ref[...], clng_ref[...], clnb_ref[...]))
        zc = bzc_ref[0, r0:r0 + row_chunk, SHORT_W:SHORT_W + CONF_W].astype(F32)
        act_c[r0:r0 + row_chunk] = (hc * zc).astype(BF16)

    br_a = jnp.dot(ga_ref[0], wpa_ref[...], preferred_element_type=F32)
    br_b = jnp.dot(act_b[...], wps_ref[...], preferred_element_type=F32)
    br_c = jnp.dot(act_c[...], wpc_ref[...], preferred_element_type=F32) + bpc_ref[...]
    merged = (gm_ref[0, :, 0:D_MODEL].astype(F32) * br_a
              + gm_ref[0, :, D_MODEL:2 * D_MODEL].astype(F32) * br_b
              + gm_ref[0, :, 2 * D_MODEL:3 * D_MODEL].astype(F32) * br_c)
    out = jnp.dot(merged.astype(BF16), wo_ref[...], preferred_element_type=F32)
    gate = mod_ref[0, :, 2 * D_MODEL:3 * D_MODEL]
    o_ref[0] = _layer_norm(alpha * h_ref[0] + gate * out, plg_ref[...], plb_ref[...])


def _merge(gated, yc, bzc, gm, h, mod, w, tile, row_chunk):
    n_batch, seq, _ = h.shape
    halo_blocks = tile // HALO
    n_halo = seq // HALO
    tok = lambda cols: pl.BlockSpec((1, tile, cols), lambda b, i: (b, i, 0))
    const = lambda shape: pl.BlockSpec(shape, lambda b, i: (0,) * len(shape))
    conv_w = SHORT_W + CONF_W
    alpha = (2.0 * DEPTH) ** 0.25
    return pl.pallas_call(
        functools.partial(_merge_kernel, tile, row_chunk, alpha),
        grid=(n_batch, seq // tile),
        in_specs=[
            tok(ATTN_W),
            tok(conv_w),
            pl.BlockSpec((1, HALO, conv_w), lambda b, i: (b, jnp.maximum(i * halo_blocks - 1, 0), 0)),
            pl.BlockSpec((1, HALO, conv_w),
                         lambda b, i: (b, jnp.minimum((i + 1) * halo_blocks, n_halo - 1), 0)),
            tok(conv_w),
            tok(N_BRANCH * D_MODEL),
            tok(D_MODEL),
            pl.BlockSpec((1, 1, 3 * D_MODEL), lambda b, i: (b, 0, 0)),
            const((ATTN_W, D_MODEL)), const((SHORT_W, D_MODEL)), const((CONF_W, D_MODEL)),
            const((D_MODEL, D_MODEL)),
            const((SHORT_K, SHORT_W)), const((CONF_K, CONF_W)), const((1, CONF_W)),
            const((1, CONF_W)), const((1, CONF_W)), const((1, D_MODEL)),
            const((1, D_MODEL)), const((1, D_MODEL)),
        ],
        out_specs=tok(D_MODEL),
        out_shape=jax.ShapeDtypeStruct((n_batch, seq, D_MODEL), F32),
        scratch_shapes=[
            pltpu.VMEM((tile + 2 * HALO, conv_w), F32),
            pltpu.VMEM((tile, SHORT_W), BF16),
            pltpu.VMEM((tile, CONF_W), BF16),
        ],
        compiler_params=pltpu.CompilerParams(
            dimension_semantics=("arbitrary", "arbitrary"), vmem_limit_bytes=VMEM_LIMIT_DEFAULT),
        name="merge",
    )(gated, yc, yc, yc, bzc, gm, h, mod, *w)


def _rope_tables(n_tokens):
    n_rows = n_tokens // GRID_W
    rows = jnp.repeat(jnp.arange(n_rows, dtype=F32), GRID_W)
    cols = jnp.tile(jnp.arange(GRID_W, dtype=F32), n_rows)
    quarter = HEAD_DIM // 4
    freqs = ROPE_THETA ** (-jnp.arange(quarter, dtype=F32) / quarter)
    ang_r = rows[:, None] * freqs
    ang_c = cols[:, None] * freqs
    cos = jnp.concatenate([jnp.cos(ang_r), jnp.cos(ang_r), jnp.cos(ang_c), jnp.cos(ang_c)], axis=-1)
    sin = jnp.concatenate([-jnp.sin(ang_r), jnp.sin(ang_r), -jnp.sin(ang_c), jnp.sin(ang_c)], axis=-1)
    return jnp.concatenate([cos, cos, sin, sin], axis=-1)


def _identity_tables(n_tokens):
    return jnp.concatenate([jnp.ones((n_tokens, LANES), F32), jnp.zeros((n_tokens, LANES), F32)], axis=-1)


def kernel(x, c, ctx, c_ctx, w_mod, b_mod, w_in, q_gain, k_gain, w_proj_attn, w_short_conv, w_proj_short, w_conf_dw, b_conf_dw, conf_ln_g, conf_ln_b, w_proj_conf, b_proj_conf, w_out, post_ln_g, post_ln_b):
    n_batch, seq, _ = x.shape
    n_ctx = ctx.shape[1]
    lat_tile = 512
    ctx_tile = n_ctx
    attn_tile = 256
    merge_tile = 256
    row_chunk = 64

    tabs_lat = _rope_tables(seq)
    tabs_ctx = _identity_tables(n_ctx)

    c_all = jnp.concatenate(
        [c, c_ctx[None, :], jnp.zeros((MOD_ROWS - n_batch - 1, D_MODEL), F32)], axis=0)
    mod_all = _modulation(c_all, w_mod, b_mod)

    h_lat, h_ctx = x, ctx
    for l in range(DEPTH):
        last = l == DEPTH - 1
        mod_lat = mod_all[l, :n_batch][:, None, :]
        mod_ctx = jnp.broadcast_to(mod_all[l, n_batch][None, None, :], (n_batch, 1, 3 * D_MODEL))
        w_bf = w_in[l].astype(BF16)
        qg = jnp.tile(q_gain[l], 2)[None, :]
        kg = jnp.tile(k_gain[l], 2)[None, :]
        branch_w = (w_proj_attn[l].astype(BF16), w_proj_short[l].astype(BF16),
                    w_proj_conf[l].astype(BF16), w_out[l].astype(BF16),
                    w_short_conv[l], w_conf_dw[l], b_conf_dw[l][None, :],
                    conf_ln_g[l][None, :], conf_ln_b[l][None, :], b_proj_conf[l][None, :],
                    post_ln_g[l][None, :], post_ln_b[l][None, :])

        if last:
            k_c, v_c = _project_kv(h_ctx, mod_ctx, w_bf[:, _OFF["k"][0]:_OFF["v"][1]],
                                   tabs_ctx, kg, ctx_tile)
        else:
            q_c, k_c, v_c, za_c, yc_c, bzc_c, gm_c = _project(
                h_ctx, mod_ctx, w_bf, tabs_ctx, qg, kg, ctx_tile)

        q_l, k_l, v_l, za_l, yc_l, bzc_l, gm_l = _project(
            h_lat, mod_lat, w_bf, tabs_lat, qg, kg, lat_tile)
        k4 = _lane_half_slots(jnp.concatenate([k_c, k_l], axis=1))
        v4 = _lane_half_slots(jnp.concatenate([v_c, v_l], axis=1))
        gated_l = _attend(q_l, k4, v4, za_l, attn_tile)
        h_lat_new = _merge(gated_l, yc_l, bzc_l, gm_l, h_lat, mod_lat, branch_w,
                           merge_tile, row_chunk)

        if not last:
            gated_c = _attend(q_c, _lane_half_slots(k_c), _lane_half_slots(v_c), za_c, ctx_tile)
            h_ctx = _merge(gated_c, yc_c, bzc_c, gm_c, h_ctx, mod_ctx, branch_w,
                           ctx_tile, row_chunk)
        h_lat = h_lat_new
    return h_lat
```

```python
import functools
import math

import jax
import jax.numpy as jnp
from jax import lax
from jax.experimental import pallas as pl
from jax.experimental.pallas import tpu as pltpu

D_MODEL = 1024
DEPTH = 2
GRID_W = 64
N_HEADS = 8
N_KV_HEADS = 2
HEAD_DIM = 64
GROUP = N_HEADS // N_KV_HEADS
ATTN_W = N_HEADS * HEAD_DIM
KV_W = N_KV_HEADS * HEAD_DIM
SHORT_W = 512
SHORT_K = 3
CONF_W = 512
CONF_K = 31
N_BRANCH = 3
ROPE_THETA = 10000.0
RMS_EPS = 1e-6
LN_EPS = 1e-5
D_IN = 2 * ATTN_W + 2 * KV_W + 4 * SHORT_W + 3 * CONF_W + N_BRANCH * D_MODEL

_OFF = {}
_o = 0
for _name, _w in (("q", ATTN_W), ("k", KV_W), ("v", KV_W), ("z_a", ATTN_W), ("s_b", SHORT_W),
                  ("s_c", SHORT_W), ("s_h", SHORT_W), ("z_s", SHORT_W), ("c_a", CONF_W),
                  ("c_g", CONF_W), ("z_c", CONF_W), ("g_m", N_BRANCH * D_MODEL)):
    _OFF[_name] = (_o, _o + _w)
    _o += _w

LANES = 128
SUBLANES = 8
HALO = 16
MOD_ROWS = 24
N_SLOTS = 2 * N_KV_HEADS
V_SLOT_W = 2 * LANES
VMEM_LIMIT_PROJECT = 56 * 1024 * 1024
VMEM_LIMIT_DEFAULT = 48 * 1024 * 1024
Q_SCALE = (HEAD_DIM ** -0.5) * math.log2(math.e)

F32 = jnp.float32
BF16 = jnp.bfloat16


def _sigmoid(x):
    return 1.0 / (1.0 + jnp.exp(-x))


def _silu(x):
    return x * _sigmoid(x)


def _mod_kernel(c_ref, w_ref, b_ref, o_ref):
    a = _silu(c_ref[...])
    o_ref[0] = jnp.dot(a, w_ref[0], precision=lax.Precision.HIGHEST,
                       preferred_element_type=F32) + b_ref[0]


def _modulation(c_all, w_mod, b_mod):
    return pl.pallas_call(
        _mod_kernel,
        grid=(DEPTH, 3),
        in_specs=[
            pl.BlockSpec((MOD_ROWS, D_MODEL), lambda l, j: (0, 0)),
            pl.BlockSpec((1, D_MODEL, D_MODEL), lambda l, j: (l, 0, j)),
            pl.BlockSpec((1, 1, D_MODEL), lambda l, j: (l, 0, j)),
        ],
        out_specs=pl.BlockSpec((1, MOD_ROWS, D_MODEL), lambda l, j: (l, 0, j)),
        out_shape=jax.ShapeDtypeStruct((DEPTH, MOD_ROWS, 3 * D_MODEL), F32),
        compiler_params=pltpu.CompilerParams(
            dimension_semantics=("arbitrary", "arbitrary"), vmem_limit_bytes=VMEM_LIMIT_DEFAULT),
        name="modulation",
    )(c_all, w_mod, b_mod.reshape(DEPTH, 1, 3 * D_MODEL))


def _norm_rope(x, gain, cos, sin):
    lane = lax.broadcasted_iota(jnp.int32, x.shape, 1)
    first = lane < HEAD_DIM
    sq = x * x
    s_lo = jnp.sum(jnp.where(first, sq, 0.0), axis=-1, keepdims=True)
    s_hi = jnp.sum(jnp.where(first, 0.0, sq), axis=-1, keepdims=True)
    ms = jnp.where(first, s_lo, s_hi) * (1.0 / HEAD_DIM)
    xn = x * lax.rsqrt(ms + RMS_EPS) * gain
    quarter = HEAD_DIM // 4
    up = pltpu.roll(xn, LANES - quarter, 1)
    down = pltpu.roll(xn, quarter, 1)
    partner = jnp.where((lane & (2 * quarter - 1)) < quarter, up, down)
    return xn * cos + partner * sin


def _modulate(h_ref, mod_ref):
    shift = mod_ref[0, :, 0:D_MODEL]
    scale = mod_ref[0, :, D_MODEL:2 * D_MODEL]
    return (h_ref[0] * (1.0 + scale) + shift).astype(BF16)


def _store_kv_slots(k, v, k4_ref, v4_ref):
    first = lax.broadcasted_iota(jnp.int32, k.shape, 1) < HEAD_DIM
    ones = jnp.ones(k.shape, BF16)
    for t, ref in ((k, k4_ref), (v, v4_ref)):
        swapped = pltpu.roll(t, HEAD_DIM, 1)
        slots = (jnp.where(first, t, 0.0), jnp.where(first, 0.0, swapped),
                 jnp.where(first, swapped, 0.0), jnp.where(first, 0.0, t))
        for n, slot in enumerate(slots):
            ref[0, n, :, 0:LANES] = slot.astype(BF16)
    for n in range(N_SLOTS):
        v4_ref[0, n, :, LANES:V_SLOT_W] = ones


def _project_kernel(h_ref, mod_ref, w_ref, tab_ref, qg_ref, kg_ref,
                    q_ref, k4_ref, v4_ref, za_ref, yc_ref, bzc_ref, gm_ref):
    u = _modulate(h_ref, mod_ref)
    cos = tab_ref[:, 0:LANES]
    sin = tab_ref[:, LANES:2 * LANES]

    def proj(lo, hi):
        return jnp.dot(u, w_ref[:, lo:hi], preferred_element_type=F32)

    def named(name):
        return proj(*_OFF[name])

    x = named("q")
    for j in range(ATTN_W // LANES):
        q = _norm_rope(x[:, j * LANES:(j + 1) * LANES], qg_ref[...], cos, sin) * Q_SCALE
        q_ref[0, :, j * LANES:(j + 1) * LANES] = q.astype(BF16)
    kv = proj(_OFF["k"][0], _OFF["v"][1])
    k = _norm_rope(kv[:, 0:KV_W], kg_ref[...], cos, sin)
    _store_kv_slots(k, kv[:, KV_W:2 * KV_W], k4_ref, v4_ref)
    za_ref[0] = _silu(named("z_a")).astype(BF16)
    yc_ref[0, :, 0:SHORT_W] = (named("s_c") * named("s_h")).astype(BF16)
    yc_ref[0, :, SHORT_W:SHORT_W + CONF_W] = (named("c_a") * _sigmoid(named("c_g"))).astype(BF16)
    bzc_ref[0, :, 0:SHORT_W] = (named("s_b") * _silu(named("z_s"))).astype(BF16)
    bzc_ref[0, :, SHORT_W:SHORT_W + CONF_W] = _silu(named("z_c")).astype(BF16)
    g_lo = _OFF["g_m"][0]
    for j in range(N_BRANCH):
        gm_ref[0, :, j * D_MODEL:(j + 1) * D_MODEL] = _sigmoid(
            proj(g_lo + j * D_MODEL, g_lo + (j + 1) * D_MODEL)).astype(BF16)


def _project_kv_kernel(h_ref, mod_ref, w_ref, tab_ref, kg_ref, k4_ref, v4_ref):
    u = _modulate(h_ref, mod_ref)
    cos = tab_ref[:, 0:LANES]
    sin = tab_ref[:, LANES:2 * LANES]
    kv = jnp.dot(u, w_ref[...], preferred_element_type=F32)
    k = _norm_rope(kv[:, 0:KV_W], kg_ref[...], cos, sin)
    _store_kv_slots(k, kv[:, KV_W:2 * KV_W], k4_ref, v4_ref)


def _slot_spec(tm, width):
    return pl.BlockSpec((1, N_SLOTS, tm, width), lambda b, i: (b, 0, i, 0))


def _project(h, mod, w_bf, tabs, q_gain, k_gain, tm):
    n_batch, seq, _ = h.shape
    tok = lambda cols: pl.BlockSpec((1, tm, cols), lambda b, i: (b, i, 0))
    flat = lambda cols: jax.ShapeDtypeStruct((n_batch, seq, cols), BF16)
    slots = lambda width: jax.ShapeDtypeStruct((n_batch, N_SLOTS, seq, width), BF16)
    return pl.pallas_call(
        _project_kernel,
        grid=(n_batch, seq // tm),
        in_specs=[
            tok(D_MODEL),
            pl.BlockSpec((1, 1, 3 * D_MODEL), lambda b, i: (b, 0, 0)),
            pl.BlockSpec((D_MODEL, D_IN), lambda b, i: (0, 0), pipeline_mode=pl.Buffered(1)),
            pl.BlockSpec((tm, 2 * LANES), lambda b, i: (i, 0)),
            pl.BlockSpec((1, LANES), lambda b, i: (0, 0)),
            pl.BlockSpec((1, LANES), lambda b, i: (0, 0)),
        ],
        out_specs=[tok(ATTN_W), _slot_spec(tm, LANES), _slot_spec(tm, V_SLOT_W), tok(ATTN_W),
                   tok(SHORT_W + CONF_W), tok(SHORT_W + CONF_W), tok(N_BRANCH * D_MODEL)],
        out_shape=[flat(ATTN_W), slots(LANES), slots(V_SLOT_W), flat(ATTN_W),
                   flat(SHORT_W + CONF_W), flat(SHORT_W + CONF_W), flat(N_BRANCH * D_MODEL)],
        compiler_params=pltpu.CompilerParams(
            dimension_semantics=("arbitrary", "arbitrary"), vmem_limit_bytes=VMEM_LIMIT_PROJECT),
        name="project",
    )(h, mod, w_bf, tabs, q_gain, k_gain)


def _project_kv(h, mod, w_kv_bf, tabs, k_gain, tm):
    n_batch, seq, _ = h.shape
    slots = lambda width: jax.ShapeDtypeStruct((n_batch, N_SLOTS, seq, width), BF16)
    return pl.pallas_call(
        _project_kv_kernel,
        grid=(n_batch, seq // tm),
        in_specs=[
            pl.BlockSpec((1, tm, D_MODEL), lambda b, i: (b, i, 0)),
            pl.BlockSpec((1, 1, 3 * D_MODEL), lambda b, i: (b, 0, 0)),
            pl.BlockSpec((D_MODEL, 2 * KV_W), lambda b, i: (0, 0)),
            pl.BlockSpec((tm, 2 * LANES), lambda b, i: (i, 0)),
            pl.BlockSpec((1, LANES), lambda b, i: (0, 0)),
        ],
        out_specs=[_slot_spec(tm, LANES), _slot_spec(tm, V_SLOT_W)],
        out_shape=[slots(LANES), slots(V_SLOT_W)],
        compiler_params=pltpu.CompilerParams(
            dimension_semantics=("arbitrary", "arbitrary"), vmem_limit_bytes=VMEM_LIMIT_DEFAULT),
        name="project_kv",
    )(h, mod, w_kv_bf, tabs, k_gain)


def _attend_kernel(n_sets, q_ref, *refs):
    k_refs = refs[0:2 * n_sets:2]
    v_refs = refs[1:2 * n_sets:2]
    za_ref, o_ref = refs[2 * n_sets:]
    for j in range(ATTN_W // LANES):
        kv = (2 * j) // GROUP
        qp = q_ref[0, :, j * LANES:(j + 1) * LANES]
        acc = None
        for half in range(2):
            slot = 2 * kv + half
            scores = [lax.dot_general(qp, k_ref[0, slot], (((1,), (1,)), ((), ())),
                                      preferred_element_type=F32) for k_ref in k_refs]
            m = None
            for s in scores:
                ms = jnp.max(s, axis=-1, keepdims=True)
                m = ms if m is None else jnp.maximum(m, ms)
            o = None
            for s, v_ref in zip(scores, v_refs):
                part = jnp.dot(jnp.exp2(s - m).astype(BF16), v_ref[0, slot],
                               preferred_element_type=F32)
                o = part if o is None else o + part
            o = o[:, 0:LANES] * (1.0 / o[:, LANES:V_SLOT_W])
            acc = o if acc is None else acc + o
        za = za_ref[0, :, j * LANES:(j + 1) * LANES].astype(F32)
        o_ref[0, :, j * LANES:(j + 1) * LANES] = (acc * za).astype(BF16)


def _attend(q, kv_sets, za, tq):
    n_batch, seq, _ = q.shape
    tok = pl.BlockSpec((1, tq, ATTN_W), lambda b, i: (b, i, 0))
    specs, operands = [], []
    for k4, v4 in kv_sets:
        n_keys = k4.shape[2]
        specs.append(pl.BlockSpec((1, N_SLOTS, n_keys, LANES), lambda b, i: (b, 0, 0, 0)))
        specs.append(pl.BlockSpec((1, N_SLOTS, n_keys, V_SLOT_W), lambda b, i: (b, 0, 0, 0)))
        operands += [k4, v4]
    return pl.pallas_call(
        functools.partial(_attend_kernel, len(kv_sets)),
        grid=(n_batch, seq // tq),
        in_specs=[tok] + specs + [tok],
        out_specs=tok,
        out_shape=jax.ShapeDtypeStruct((n_batch, seq, ATTN_W), BF16),
        compiler_params=pltpu.CompilerParams(
            dimension_semantics=("arbitrary", "arbitrary"), vmem_limit_bytes=VMEM_LIMIT_DEFAULT),
        name="attend",
    )(q, *operands, za)


def _layer_norm(x, g, b):
    mu = jnp.mean(x, axis=-1, keepdims=True)
    xc = x - mu
    var = jnp.mean(xc * xc, axis=-1, keepdims=True)
    return xc * lax.rsqrt(var + LN_EPS) * g + b


def _merge_kernel(tile, row_chunk, alpha,
                  ga_ref, yc_ref, ycp_ref, ycn_ref, bzc_ref, gm_ref, h_ref, mod_ref,
                  wpa_ref, wps_ref, wpc_ref, wo_ref, wsc_ref, wcdw_ref, bcdw_ref,
                  clng_ref, clnb_ref, bpc_ref, plg_ref, plb_ref,
                  o_ref, xbuf, xs, act_b, act_c):
    i = pl.program_id(1)
    n_tiles = pl.num_programs(1)
    conv_w = SHORT_W + CONF_W
    groups = row_chunk // SUBLANES

    xbuf[0:HALO] = ycp_ref[0].astype(F32)
    xbuf[HALO:HALO + tile] = yc_ref[0].astype(F32)
    xbuf[HALO + tile:HALO + tile + HALO] = ycn_ref[0].astype(F32)

    @pl.when(i == 0)
    def _():
        xbuf[0:HALO] = jnp.zeros((HALO, conv_w), F32)

    @pl.when(i == n_tiles - 1)
    def _():
        xbuf[HALO + tile:HALO + tile + HALO] = jnp.zeros((HALO, conv_w), F32)

    xs_rows = xs.shape[1]
    for r in range(SUBLANES):
        xs[r] = xbuf[r:r + xs_rows, SHORT_W:conv_w]

    for r0 in range(0, tile, row_chunk):
        acc = None
        for k in range(SHORT_K):
            start = HALO + r0 + k - SHORT_K // 2
            term = xbuf[start:start + row_chunk, 0:SHORT_W] * wsc_ref[k:k + 1, :]
            acc = term if acc is None else acc + term
        bz = bzc_ref[0, r0:r0 + row_chunk, 0:SHORT_W].astype(F32)
        act_b[r0:r0 + row_chunk] = (acc * bz).astype(BF16)

        acc = None
        for k in range(CONF_K):
            start = HALO + r0 + k - CONF_K // 2
            r = start % SUBLANES
            slab = xs[r, start - r:start - r + row_chunk, :]
            term = slab.reshape(groups, SUBLANES, CONF_W) * wcdw_ref[k]
            acc = term if acc is None else acc + term
        conv = acc.reshape(row_chunk, CONF_W) + bcdw_ref[...]
        hc = _silu(_layer_norm(conv, clng_ref[...], clnb_ref[...]))
        zc = bzc_ref[0, r0:r0 + row_chunk, SHORT_W:conv_w].astype(F32)
        act_c[r0:r0 + row_chunk] = (hc * zc).astype(BF16)

    br_a = jnp.dot(ga_ref[0], wpa_ref[...], preferred_element_type=F32)
    br_b = jnp.dot(act_b[...], wps_ref[...], preferred_element_type=F32)
    br_c = jnp.dot(act_c[...], wpc_ref[...], preferred_element_type=F32) + bpc_ref[...]
    merged = (gm_ref[0, :, 0:D_MODEL].astype(F32) * br_a
              + gm_ref[0, :, D_MODEL:2 * D_MODEL].astype(F32) * br_b
              + gm_ref[0, :, 2 * D_MODEL:3 * D_MODEL].astype(F32) * br_c)
    out = jnp.dot(merged.astype(BF16), wo_ref[...], preferred_element_type=F32)
    gate = mod_ref[0, :, 2 * D_MODEL:3 * D_MODEL]
    o_ref[0] = _layer_norm(alpha * h_ref[0] + gate * out, plg_ref[...], plb_ref[...])


def _merge(gated, yc, bzc, gm, h, mod, w, tile, row_chunk):
    n_batch, seq, _ = h.shape
    halo_blocks = tile // HALO
    n_halo = seq // HALO
    tok = lambda cols: pl.BlockSpec((1, tile, cols), lambda b, i: (b, i, 0))
    const = lambda shape: pl.BlockSpec(shape, lambda b, i: (0,) * len(shape))
    conv_w = SHORT_W + CONF_W
    alpha = (2.0 * DEPTH) ** 0.25
    xs_rows = tile + HALO + SUBLANES
    return pl.pallas_call(
        functools.partial(_merge_kernel, tile, row_chunk, alpha),
        grid=(n_batch, seq // tile),
        in_specs=[
            tok(ATTN_W),
            tok(conv_w),
            pl.BlockSpec((1, HALO, conv_w), lambda b, i: (b, jnp.maximum(i * halo_blocks - 1, 0), 0)),
            pl.BlockSpec((1, HALO, conv_w),
                         lambda b, i: (b, jnp.minimum((i + 1) * halo_blocks, n_halo - 1), 0)),
            tok(conv_w),
            tok(N_BRANCH * D_MODEL),
            tok(D_MODEL),
            pl.BlockSpec((1, 1, 3 * D_MODEL), lambda b, i: (b, 0, 0)),
            const((ATTN_W, D_MODEL)), const((SHORT_W, D_MODEL)), const((CONF_W, D_MODEL)),
            const((D_MODEL, D_MODEL)),
            const((SHORT_K, SHORT_W)), const((CONF_K, SUBLANES, CONF_W)), const((1, CONF_W)),
            const((1, CONF_W)), const((1, CONF_W)), const((1, D_MODEL)),
            const((1, D_MODEL)), const((1, D_MODEL)),
        ],
        out_specs=tok(D_MODEL),
        out_shape=jax.ShapeDtypeStruct((n_batch, seq, D_MODEL), F32),
        scratch_shapes=[
            pltpu.VMEM((tile + 2 * HALO, conv_w), F32),
            pltpu.VMEM((SUBLANES, xs_rows, CONF_W), F32),
            pltpu.VMEM((tile, SHORT_W), BF16),
            pltpu.VMEM((tile, CONF_W), BF16),
        ],
        compiler_params=pltpu.CompilerParams(
            dimension_semantics=("arbitrary", "arbitrary"), vmem_limit_bytes=VMEM_LIMIT_DEFAULT),
        name="merge",
    )(gated, yc, yc, yc, bzc, gm, h, mod, *w)


def _rope_tables(n_tokens):
    n_rows = n_tokens // GRID_W
    rows = jnp.repeat(jnp.arange(n_rows, dtype=F32), GRID_W)
    cols = jnp.tile(jnp.arange(GRID_W, dtype=F32), n_rows)
    quarter = HEAD_DIM // 4
    freqs = ROPE_THETA ** (-jnp.arange(quarter, dtype=F32) / quarter)
    ang_r = rows[:, None] * freqs
    ang_c = cols[:, None] * freqs
    cos = jnp.concatenate([jnp.cos(ang_r), jnp.cos(ang_r), jnp.cos(ang_c), jnp.cos(ang_c)], axis=-1)
    sin = jnp.concatenate([-jnp.sin(ang_r), jnp.sin(ang_r), -jnp.sin(ang_c), jnp.sin(ang_c)], axis=-1)
    return jnp.concatenate([cos, cos, sin, sin], axis=-1)


def _identity_tables(n_tokens):
    return jnp.concatenate([jnp.ones((n_tokens, LANES), F32), jnp.zeros((n_tokens, LANES), F32)], axis=-1)


def kernel(x, c, ctx, c_ctx, w_mod, b_mod, w_in, q_gain, k_gain, w_proj_attn, w_short_conv, w_proj_short, w_conf_dw, b_conf_dw, conf_ln_g, conf_ln_b, w_proj_conf, b_proj_conf, w_out, post_ln_g, post_ln_b):
    n_batch, seq, _ = x.shape
    n_ctx = ctx.shape[1]
    lat_tile = 512
    ctx_tile = n_ctx
    attn_tile = 512
    merge_tile = 512
    row_chunk = 64

    tabs_lat = _rope_tables(seq)
    tabs_ctx = _identity_tables(n_ctx)

    c_all = jnp.concatenate(
        [c, c_ctx[None, :], jnp.zeros((MOD_ROWS - n_batch - 1, D_MODEL), F32)], axis=0)
    mod_all = _modulation(c_all, w_mod, b_mod)

    h_lat, h_ctx = x, ctx
    for l in range(DEPTH):
        last = l == DEPTH - 1
        mod_lat = mod_all[l, :n_batch][:, None, :]
        mod_ctx = jnp.broadcast_to(mod_all[l, n_batch][None, None, :], (n_batch, 1, 3 * D_MODEL))
        w_bf = w_in[l].astype(BF16)
        qg = jnp.tile(q_gain[l], 2)[None, :]
        kg = jnp.tile(k_gain[l], 2)[None, :]
        branch_w = (w_proj_attn[l].astype(BF16), w_proj_short[l].astype(BF16),
                    w_proj_conf[l].astype(BF16), w_out[l].astype(BF16),
                    w_short_conv[l],
                    jnp.broadcast_to(w_conf_dw[l][:, None, :], (CONF_K, SUBLANES, CONF_W)),
                    b_conf_dw[l][None, :],
                    conf_ln_g[l][None, :], conf_ln_b[l][None, :], b_proj_conf[l][None, :],
                    post_ln_g[l][None, :], post_ln_b[l][None, :])

        if last:
            k4_c, v4_c = _project_kv(h_ctx, mod_ctx, w_bf[:, _OFF["k"][0]:_OFF["v"][1]],
                                     tabs_ctx, kg, ctx_tile)
        else:
            q_c, k4_c, v4_c, za_c, yc_c, bzc_c, gm_c = _project(
                h_ctx, mod_ctx, w_bf, tabs_ctx, qg, kg, ctx_tile)

        q_l, k4_l, v4_l, za_l, yc_l, bzc_l, gm_l = _project(
            h_lat, mod_lat, w_bf, tabs_lat, qg, kg, lat_tile)
        gated_l = _attend(q_l, ((k4_c, v4_c), (k4_l, v4_l)), za_l, attn_tile)
        h_lat_new = _merge(gated_l, yc_l, bzc_l, gm_l, h_lat, mod_lat, branch_w,
                           merge_tile, row_chunk)

        if not last:
            gated_c = _attend(q_c, ((k4_c, v4_c),), za_c, ctx_tile)
            h_ctx = _merge(gated_c, yc_c, bzc_c, gm_c, h_ctx, mod_ctx, branch_w,
                           ctx_tile, row_chunk)
        h_lat = h_lat_new
    return h_lat
```

```python
import functools
import math

import jax
import jax.numpy as jnp
from jax import lax
from jax.experimental import pallas as pl
from jax.experimental.pallas import tpu as pltpu

D_MODEL = 1024
DEPTH = 2
GRID_W = 64
N_HEADS = 8
N_KV_HEADS = 2
HEAD_DIM = 64
GROUP = N_HEADS // N_KV_HEADS
ATTN_W = N_HEADS * HEAD_DIM
KV_W = N_KV_HEADS * HEAD_DIM
SHORT_W = 512
SHORT_K = 3
CONF_W = 512
CONF_K = 31
N_BRANCH = 3
ROPE_THETA = 10000.0
RMS_EPS = 1e-6
LN_EPS = 1e-5
D_IN = 2 * ATTN_W + 2 * KV_W + 4 * SHORT_W + 3 * CONF_W + N_BRANCH * D_MODEL

_OFF = {}
_o = 0
for _name, _w in (("q", ATTN_W), ("k", KV_W), ("v", KV_W), ("z_a", ATTN_W), ("s_b", SHORT_W),
                  ("s_c", SHORT_W), ("s_h", SHORT_W), ("z_s", SHORT_W), ("c_a", CONF_W),
                  ("c_g", CONF_W), ("z_c", CONF_W), ("g_m", N_BRANCH * D_MODEL)):
    _OFF[_name] = (_o, _o + _w)
    _o += _w

LANES = 128
SUBLANES = 8
HALO = 16
MOD_ROWS = 24
N_SLOTS = 2 * N_KV_HEADS
V_SLOT_W = 2 * LANES
VMEM_LIMIT_PROJECT = 56 * 1024 * 1024
VMEM_LIMIT_DEFAULT = 48 * 1024 * 1024
Q_SCALE = (HEAD_DIM ** -0.5) * math.log2(math.e)

F32 = jnp.float32
BF16 = jnp.bfloat16


def _sigmoid(x):
    return 1.0 / (1.0 + jnp.exp(-x))


def _silu(x):
    return x * _sigmoid(x)


def _mod_kernel(c_ref, w_ref, b_ref, o_ref):
    a = _silu(c_ref[...])
    o_ref[0] = jnp.dot(a, w_ref[0], precision=lax.Precision.HIGHEST,
                       preferred_element_type=F32) + b_ref[0]


def _modulation(c_all, w_mod, b_mod):
    return pl.pallas_call(
        _mod_kernel,
        grid=(DEPTH, 3),
        in_specs=[
            pl.BlockSpec((MOD_ROWS, D_MODEL), lambda l, j: (0, 0)),
            pl.BlockSpec((1, D_MODEL, D_MODEL), lambda l, j: (l, 0, j)),
            pl.BlockSpec((1, 1, D_MODEL), lambda l, j: (l, 0, j)),
        ],
        out_specs=pl.BlockSpec((1, MOD_ROWS, D_MODEL), lambda l, j: (l, 0, j)),
        out_shape=jax.ShapeDtypeStruct((DEPTH, MOD_ROWS, 3 * D_MODEL), F32),
        compiler_params=pltpu.CompilerParams(
            dimension_semantics=("arbitrary", "arbitrary"), vmem_limit_bytes=VMEM_LIMIT_DEFAULT),
        name="modulation",
    )(c_all, w_mod, b_mod.reshape(DEPTH, 1, 3 * D_MODEL))


def _norm_rope(x, gain, cos, sin):
    lane = lax.broadcasted_iota(jnp.int32, x.shape, 1)
    first = lane < HEAD_DIM
    sq = x * x
    s_lo = jnp.sum(jnp.where(first, sq, 0.0), axis=-1, keepdims=True)
    s_hi = jnp.sum(jnp.where(first, 0.0, sq), axis=-1, keepdims=True)
    ms = jnp.where(first, s_lo, s_hi) * (1.0 / HEAD_DIM)
    xn = x * lax.rsqrt(ms + RMS_EPS) * gain
    quarter = HEAD_DIM // 4
    up = pltpu.roll(xn, LANES - quarter, 1)
    down = pltpu.roll(xn, quarter, 1)
    partner = jnp.where((lane & (2 * quarter - 1)) < quarter, up, down)
    return xn * cos + partner * sin


def _modulate(h_ref, mod_ref):
    shift = mod_ref[0, :, 0:D_MODEL]
    scale = mod_ref[0, :, D_MODEL:2 * D_MODEL]
    return (h_ref[0] * (1.0 + scale) + shift).astype(BF16)


def _store_kv_slots(k, v, k4_ref, v4_ref):
    first = lax.broadcasted_iota(jnp.int32, k.shape, 1) < HEAD_DIM
    ones = jnp.ones(k.shape, BF16)
    for t, ref in ((k, k4_ref), (v, v4_ref)):
        swapped = pltpu.roll(t, HEAD_DIM, 1)
        slots = (jnp.where(first, t, 0.0), jnp.where(first, 0.0, swapped),
                 jnp.where(first, swapped, 0.0), jnp.where(first, 0.0, t))
        for n, slot in enumerate(slots):
            ref[0, n, :, 0:LANES] = slot.astype(BF16)
    for n in range(N_SLOTS):
        v4_ref[0, n, :, LANES:V_SLOT_W] = ones


def _project_kernel(h_ref, mod_ref, w_ref, tab_ref, qg_ref, kg_ref,
                    q_ref, k4_ref, v4_ref, za_ref, yc_ref, bzc_ref, gm_ref):
    u = _modulate(h_ref, mod_ref)
    cos = tab_ref[:, 0:LANES]
    sin = tab_ref[:, LANES:2 * LANES]

    def proj(lo, hi):
        return jnp.dot(u, w_ref[:, lo:hi], preferred_element_type=F32)

    def named(name):
        return proj(*_OFF[name])

    x = named("q")
    for j in range(ATTN_W // LANES):
        q = _norm_rope(x[:, j * LANES:(j + 1) * LANES], qg_ref[...], cos, sin) * Q_SCALE
        q_ref[0, :, j * LANES:(j + 1) * LANES] = q.astype(BF16)
    kv = proj(_OFF["k"][0], _OFF["v"][1])
    k = _norm_rope(kv[:, 0:KV_W], kg_ref[...], cos, sin)
    _store_kv_slots(k, kv[:, KV_W:2 * KV_W], k4_ref, v4_ref)
    za_ref[0] = _silu(named("z_a")).astype(BF16)
    yc_ref[0, :, 0:SHORT_W] = (named("s_c") * named("s_h")).astype(BF16)
    yc_ref[0, :, SHORT_W:SHORT_W + CONF_W] = (named("c_a") * _sigmoid(named("c_g"))).astype(BF16)
    bzc_ref[0, :, 0:SHORT_W] = (named("s_b") * _silu(named("z_s"))).astype(BF16)
    bzc_ref[0, :, SHORT_W:SHORT_W + CONF_W] = _silu(named("z_c")).astype(BF16)
    g_lo = _OFF["g_m"][0]
    for j in range(N_BRANCH):
        gm_ref[0, :, j * D_MODEL:(j + 1) * D_MODEL] = _sigmoid(
            proj(g_lo + j * D_MODEL, g_lo + (j + 1) * D_MODEL)).astype(BF16)


def _project_kv_kernel(h_ref, mod_ref, w_ref, tab_ref, kg_ref, k4_ref, v4_ref):
    u = _modulate(h_ref, mod_ref)
    cos = tab_ref[:, 0:LANES]
    sin = tab_ref[:, LANES:2 * LANES]
    kv = jnp.dot(u, w_ref[...], preferred_element_type=F32)
    k = _norm_rope(kv[:, 0:KV_W], kg_ref[...], cos, sin)
    _store_kv_slots(k, kv[:, KV_W:2 * KV_W], k4_ref, v4_ref)


def _slot_spec(tm, width):
    return pl.BlockSpec((1, N_SLOTS, tm, width), lambda b, i: (b, 0, i, 0))


def _project(h, mod, w_bf, tabs, q_gain, k_gain, tm):
    n_batch, seq, _ = h.shape
    tok = lambda cols: pl.BlockSpec((1, tm, cols), lambda b, i: (b, i, 0))
    flat = lambda cols: jax.ShapeDtypeStruct((n_batch, seq, cols), BF16)
    slots = lambda width: jax.ShapeDtypeStruct((n_batch, N_SLOTS, seq, width), BF16)
    return pl.pallas_call(
        _project_kernel,
        grid=(n_batch, seq // tm),
        in_specs=[
            tok(D_MODEL),
            pl.BlockSpec((1, 1, 3 * D_MODEL), lambda b, i: (b, 0, 0)),
            pl.BlockSpec((D_MODEL, D_IN), lambda b, i: (0, 0), pipeline_mode=pl.Buffered(1)),
            pl.BlockSpec((tm, 2 * LANES), lambda b, i: (i, 0)),
            pl.BlockSpec((1, LANES), lambda b, i: (0, 0)),
            pl.BlockSpec((1, LANES), lambda b, i: (0, 0)),
        ],
        out_specs=[tok(ATTN_W), _slot_spec(tm, LANES), _slot_spec(tm, V_SLOT_W), tok(ATTN_W),
                   tok(SHORT_W + CONF_W), tok(SHORT_W + CONF_W), tok(N_BRANCH * D_MODEL)],
        out_shape=[flat(ATTN_W), slots(LANES), slots(V_SLOT_W), flat(ATTN_W),
                   flat(SHORT_W + CONF_W), flat(SHORT_W + CONF_W), flat(N_BRANCH * D_MODEL)],
        compiler_params=pltpu.CompilerParams(
            dimension_semantics=("arbitrary", "arbitrary"), vmem_limit_bytes=VMEM_LIMIT_PROJECT),
        name="project",
    )(h, mod, w_bf, tabs, q_gain, k_gain)


def _project_kv(h, mod, w_kv_bf, tabs, k_gain, tm):
    n_batch, seq, _ = h.shape
    slots = lambda width: jax.ShapeDtypeStruct((n_batch, N_SLOTS, seq, width), BF16)
    return pl.pallas_call(
        _project_kv_kernel,
        grid=(n_batch, seq // tm),
        in_specs=[
            pl.BlockSpec((1, tm, D_MODEL), lambda b, i: (b, i, 0)),
            pl.BlockSpec((1, 1, 3 * D_MODEL), lambda b, i: (b, 0, 0)),
            pl.BlockSpec((D_MODEL, 2 * KV_W), lambda b, i: (0, 0)),
            pl.BlockSpec((tm, 2 * LANES), lambda b, i: (i, 0)),
            pl.BlockSpec((1, LANES), lambda b, i: (0, 0)),
        ],
        out_specs=[_slot_spec(tm, LANES), _slot_spec(tm, V_SLOT_W)],
        out_shape=[slots(LANES), slots(V_SLOT_W)],
        compiler_params=pltpu.CompilerParams(
            dimension_semantics=("arbitrary", "arbitrary"), vmem_limit_bytes=VMEM_LIMIT_DEFAULT),
        name="project_kv",
    )(h, mod, w_kv_bf, tabs, k_gain)


def _attend_kernel(n_sets, tq, rows, q_ref, *refs):
    k_refs = refs[0:2 * n_sets:2]
    v_refs = refs[1:2 * n_sets:2]
    za_ref, o_ref = refs[2 * n_sets:]
    for r0 in range(0, tq, rows):
        for j in range(ATTN_W // LANES):
            kv = (2 * j) // GROUP
            qp = q_ref[0, r0:r0 + rows, j * LANES:(j + 1) * LANES]
            acc = None
            for half in range(2):
                slot = 2 * kv + half
                scores = [lax.dot_general(qp, k_ref[0, slot], (((1,), (1,)), ((), ())),
                                          preferred_element_type=F32) for k_ref in k_refs]
                m = None
                for s in scores:
                    ms = jnp.max(s, axis=-1, keepdims=True)
                    m = ms if m is None else jnp.maximum(m, ms)
                o = None
                for s, v_ref in zip(scores, v_refs):
                    part = jnp.dot(jnp.exp2(s - m).astype(BF16), v_ref[0, slot],
                                   preferred_element_type=F32)
                    o = part if o is None else o + part
                o = o[:, 0:LANES] * (1.0 / o[:, LANES:V_SLOT_W])
                acc = o if acc is None else acc + o
            za = za_ref[0, r0:r0 + rows, j * LANES:(j + 1) * LANES].astype(F32)
            o_ref[0, r0:r0 + rows, j * LANES:(j + 1) * LANES] = (acc * za).astype(BF16)


def _attend(q, kv_sets, za, tq, rows):
    n_batch, seq, _ = q.shape
    tok = pl.BlockSpec((1, tq, ATTN_W), lambda b, i: (b, i, 0))
    specs, operands = [], []
    for k4, v4 in kv_sets:
        n_keys = k4.shape[2]
        specs.append(pl.BlockSpec((1, N_SLOTS, n_keys, LANES), lambda b, i: (b, 0, 0, 0)))
        specs.append(pl.BlockSpec((1, N_SLOTS, n_keys, V_SLOT_W), lambda b, i: (b, 0, 0, 0)))
        operands += [k4, v4]
    return pl.pallas_call(
        functools.partial(_attend_kernel, len(kv_sets), tq, rows),
        grid=(n_batch, seq // tq),
        in_specs=[tok] + specs + [tok],
        out_specs=tok,
        out_shape=jax.ShapeDtypeStruct((n_batch, seq, ATTN_W), BF16),
        compiler_params=pltpu.CompilerParams(
            dimension_semantics=("arbitrary", "arbitrary"), vmem_limit_bytes=VMEM_LIMIT_DEFAULT),
        name="attend",
    )(q, *operands, za)


def _layer_norm(x, g, b):
    mu = jnp.mean(x, axis=-1, keepdims=True)
    xc = x - mu
    var = jnp.mean(xc * xc, axis=-1, keepdims=True)
    return xc * lax.rsqrt(var + LN_EPS) * g + b


def _merge_kernel(tile, sub, row_chunk, alpha,
                  ga_ref, yc_ref, ycp_ref, ycn_ref, bzc_ref, gm_ref, h_ref, mod_ref,
                  wpa_ref, wps_ref, wpc_ref, wo_ref, wsc_ref, wcdw_ref, bcdw_ref,
                  clng_ref, clnb_ref, bpc_ref, plg_ref, plb_ref,
                  o_ref, xbuf, xs, act_b, act_c):
    i = pl.program_id(1)
    n_tiles = pl.num_programs(1)
    conv_w = SHORT_W + CONF_W
    groups = row_chunk // SUBLANES

    xbuf[0:HALO] = ycp_ref[0].astype(F32)
    xbuf[HALO:HALO + tile] = yc_ref[0].astype(F32)
    xbuf[HALO + tile:HALO + tile + HALO] = ycn_ref[0].astype(F32)

    @pl.when(i == 0)
    def _():
        xbuf[0:HALO] = jnp.zeros((HALO, conv_w), F32)

    @pl.when(i == n_tiles - 1)
    def _():
        xbuf[HALO + tile:HALO + tile + HALO] = jnp.zeros((HALO, conv_w), F32)

    xs_rows = xs.shape[1]
    for r in range(SUBLANES):
        xs[r] = xbuf[r:r + xs_rows, SHORT_W:conv_w]

    gate = mod_ref[0, :, 2 * D_MODEL:3 * D_MODEL]
    for s0 in range(0, tile, sub):
        for r0 in range(s0, s0 + sub, row_chunk):
            acc = None
            for k in range(SHORT_K):
                start = HALO + r0 + k - SHORT_K // 2
                term = xbuf[start:start + row_chunk, 0:SHORT_W] * wsc_ref[k:k + 1, :]
                acc = term if acc is None else acc + term
            bz = bzc_ref[0, r0:r0 + row_chunk, 0:SHORT_W].astype(F32)
            act_b[r0:r0 + row_chunk] = (acc * bz).astype(BF16)

            acc = None
            for k in range(CONF_K):
                start = HALO + r0 + k - CONF_K // 2
                r = start % SUBLANES
                slab = xs[r, start - r:start - r + row_chunk, :]
                term = slab.reshape(groups, SUBLANES, CONF_W) * wcdw_ref[k]
                acc = term if acc is None else acc + term
            conv = acc.reshape(row_chunk, CONF_W) + bcdw_ref[...]
            hc = _silu(_layer_norm(conv, clng_ref[...], clnb_ref[...]))
            zc = bzc_ref[0, r0:r0 + row_chunk, SHORT_W:conv_w].astype(F32)
            act_c[r0:r0 + row_chunk] = (hc * zc).astype(BF16)

        rows = slice(s0, s0 + sub)
        br_a = jnp.dot(ga_ref[0, rows, :], wpa_ref[...], preferred_element_type=F32)
        br_b = jnp.dot(act_b[rows, :], wps_ref[...], preferred_element_type=F32)
        br_c = jnp.dot(act_c[rows, :], wpc_ref[...], preferred_element_type=F32) + bpc_ref[...]
        merged = (gm_ref[0, rows, 0:D_MODEL] * br_a.astype(BF16)
                  + gm_ref[0, rows, D_MODEL:2 * D_MODEL] * br_b.astype(BF16)
                  + gm_ref[0, rows, 2 * D_MODEL:3 * D_MODEL] * br_c.astype(BF16))
        out = jnp.dot(merged, wo_ref[...], preferred_element_type=F32)
        o_ref[0, rows, :] = _layer_norm(alpha * h_ref[0, rows, :] + gate * out,
                                        plg_ref[...], plb_ref[...])


def _merge(gated, yc, bzc, gm, h, mod, w, tile, sub, row_chunk):
    n_batch, seq, _ = h.shape
    halo_blocks = tile // HALO
    n_halo = seq // HALO
    tok = lambda cols: pl.BlockSpec((1, tile, cols), lambda b, i: (b, i, 0))
    const = lambda shape: pl.BlockSpec(shape, lambda b, i: (0,) * len(shape))
    conv_w = SHORT_W + CONF_W
    alpha = (2.0 * DEPTH) ** 0.25
    xs_rows = tile + HALO + SUBLANES
    return pl.pallas_call(
        functools.partial(_merge_kernel, tile, sub, row_chunk, alpha),
        grid=(n_batch, seq // tile),
        in_specs=[
            tok(ATTN_W),
            tok(conv_w),
            pl.BlockSpec((1, HALO, conv_w), lambda b, i: (b, jnp.maximum(i * halo_blocks - 1, 0), 0)),
            pl.BlockSpec((1, HALO, conv_w),
                         lambda b, i: (b, jnp.minimum((i + 1) * halo_blocks, n_halo - 1), 0)),
            tok(conv_w),
            tok(N_BRANCH * D_MODEL),
            tok(D_MODEL),
            pl.BlockSpec((1, 1, 3 * D_MODEL), lambda b, i: (b, 0, 0)),
            const((ATTN_W, D_MODEL)), const((SHORT_W, D_MODEL)), const((CONF_W, D_MODEL)),
            const((D_MODEL, D_MODEL)),
            const((SHORT_K, SHORT_W)), const((CONF_K, SUBLANES, CONF_W)), const((1, CONF_W)),
            const((1, CONF_W)), const((1, CONF_W)), const((1, D_MODEL)),
            const((1, D_MODEL)), const((1, D_MODEL)),
        ],
        out_specs=tok(D_MODEL),
        out_shape=jax.ShapeDtypeStruct((n_batch, seq, D_MODEL), F32),
        scratch_shapes=[
            pltpu.VMEM((tile + 2 * HALO, conv_w), F32),
            pltpu.VMEM((SUBLANES, xs_rows, CONF_W), F32),
            pltpu.VMEM((tile, SHORT_W), BF16),
            pltpu.VMEM((tile, CONF_W), BF16),
        ],
        compiler_params=pltpu.CompilerParams(
            dimension_semantics=("arbitrary", "arbitrary"), vmem_limit_bytes=VMEM_LIMIT_DEFAULT),
        name="merge",
    )(gated, yc, yc, yc, bzc, gm, h, mod, *w)


def _rope_tables(n_tokens):
    n_rows = n_tokens // GRID_W
    rows = jnp.repeat(jnp.arange(n_rows, dtype=F32), GRID_W)
    cols = jnp.tile(jnp.arange(GRID_W, dtype=F32), n_rows)
    quarter = HEAD_DIM // 4
    freqs = ROPE_THETA ** (-jnp.arange(quarter, dtype=F32) / quarter)
    ang_r = rows[:, None] * freqs
    ang_c = cols[:, None] * freqs
    cos = jnp.concatenate([jnp.cos(ang_r), jnp.cos(ang_r), jnp.cos(ang_c), jnp.cos(ang_c)], axis=-1)
    sin = jnp.concatenate([-jnp.sin(ang_r), jnp.sin(ang_r), -jnp.sin(ang_c), jnp.sin(ang_c)], axis=-1)
    return jnp.concatenate([cos, cos, sin, sin], axis=-1)


def _identity_tables(n_tokens):
    return jnp.concatenate([jnp.ones((n_tokens, LANES), F32), jnp.zeros((n_tokens, LANES), F32)], axis=-1)


def kernel(x, c, ctx, c_ctx, w_mod, b_mod, w_in, q_gain, k_gain, w_proj_attn, w_short_conv, w_proj_short, w_conf_dw, b_conf_dw, conf_ln_g, conf_ln_b, w_proj_conf, b_proj_conf, w_out, post_ln_g, post_ln_b):
    n_batch, seq, _ = x.shape
    n_ctx = ctx.shape[1]
    lat_tile = 512
    ctx_tile = n_ctx
    attn_tile = 1024
    attn_rows = 512
    merge_tile = 512
    merge_sub = 256
    row_chunk = 64

    tabs_lat = _rope_tables(seq)
    tabs_ctx = _identity_tables(n_ctx)

    c_all = jnp.concatenate(
        [c, c_ctx[None, :], jnp.zeros((MOD_ROWS - n_batch - 1, D_MODEL), F32)], axis=0)
    mod_all = _modulation(c_all, w_mod, b_mod)

    h_lat, h_ctx = x, ctx
    for l in range(DEPTH):
        last = l == DEPTH - 1
        mod_lat = mod_all[l, :n_batch][:, None, :]
        mod_ctx = jnp.broadcast_to(mod_all[l, n_batch][None, None, :], (n_batch, 1, 3 * D_MODEL))
        w_bf = w_in[l].astype(BF16)
        qg = jnp.tile(q_gain[l], 2)[None, :]
        kg = jnp.tile(k_gain[l], 2)[None, :]
        branch_w = (w_proj_attn[l].astype(BF16), w_proj_short[l].astype(BF16),
                    w_proj_conf[l].astype(BF16), w_out[l].astype(BF16),
                    w_short_conv[l],
                    jnp.broadcast_to(w_conf_dw[l][:, None, :], (CONF_K, SUBLANES, CONF_W)),
                    b_conf_dw[l][None, :],
                    conf_ln_g[l][None, :], conf_ln_b[l][None, :], b_proj_conf[l][None, :],
                    post_ln_g[l][None, :], post_ln_b[l][None, :])

        if last:
            k4_c, v4_c = _project_kv(h_ctx, mod_ctx, w_bf[:, _OFF["k"][0]:_OFF["v"][1]],
                                     tabs_ctx, kg, ctx_tile)
        else:
            q_c, k4_c, v4_c, za_c, yc_c, bzc_c, gm_c = _project(
                h_ctx, mod_ctx, w_bf, tabs_ctx, qg, kg, ctx_tile)

        q_l, k4_l, v4_l, za_l, yc_l, bzc_l, gm_l = _project(
            h_lat, mod_lat, w_bf, tabs_lat, qg, kg, lat_tile)
        gated_l = _attend(q_l, ((k4_c, v4_c), (k4_l, v4_l)), za_l, attn_tile, attn_rows)
        h_lat_new = _merge(gated_l, yc_l, bzc_l, gm_l, h_lat, mod_lat, branch_w,
                           merge_tile, merge_sub, row_chunk)

        if not last:
            gated_c = _attend(q_c, ((k4_c, v4_c),), za_c, ctx_tile, ctx_tile)
            h_ctx = _merge(gated_c, yc_c, bzc_c, gm_c, h_ctx, mod_ctx, branch_w,
                           ctx_tile, ctx_tile, row_chunk)
        h_lat = h_lat_new
    return h_lat
```

```python
import functools
import math

import jax
import jax.numpy as jnp
from jax import lax
from jax.experimental import pallas as pl
from jax.experimental.pallas import tpu as pltpu

D_MODEL = 1024
DEPTH = 2
GRID_W = 64
N_HEADS = 8
N_KV_HEADS = 2
HEAD_DIM = 64
GROUP = N_HEADS // N_KV_HEADS
ATTN_W = N_HEADS * HEAD_DIM
KV_W = N_KV_HEADS * HEAD_DIM
SHORT_W = 512
SHORT_K = 3
CONF_W = 512
CONF_K = 31
N_BRANCH = 3
ROPE_THETA = 10000.0
RMS_EPS = 1e-6
LN_EPS = 1e-5
D_IN = 2 * ATTN_W + 2 * KV_W + 4 * SHORT_W + 3 * CONF_W + N_BRANCH * D_MODEL

_OFF = {}
_o = 0
for _name, _w in (("q", ATTN_W), ("k", KV_W), ("v", KV_W), ("z_a", ATTN_W), ("s_b", SHORT_W),
                  ("s_c", SHORT_W), ("s_h", SHORT_W), ("z_s", SHORT_W), ("c_a", CONF_W),
                  ("c_g", CONF_W), ("z_c", CONF_W), ("g_m", N_BRANCH * D_MODEL)):
    _OFF[_name] = (_o, _o + _w)
    _o += _w

LANES = 128
SUBLANES = 8
HALO = 16
MOD_ROWS = 24
N_SLOTS = 2 * N_KV_HEADS
V_SLOT_W = 2 * LANES
VMEM_LIMIT_PROJECT = 56 * 1024 * 1024
VMEM_LIMIT_DEFAULT = 48 * 1024 * 1024
Q_SCALE = (HEAD_DIM ** -0.5) * math.log2(math.e)

F32 = jnp.float32
BF16 = jnp.bfloat16


def _sigmoid(x):
    return 1.0 / (1.0 + jnp.exp(-x))


def _silu(x):
    return x * _sigmoid(x)


def _mod_kernel(c_ref, w_ref, b_ref, o_ref):
    a = _silu(c_ref[...])
    o_ref[0] = jnp.dot(a, w_ref[0], precision=lax.Precision.HIGHEST,
                       preferred_element_type=F32) + b_ref[0]


def _modulation(c_all, w_mod, b_mod):
    return pl.pallas_call(
        _mod_kernel,
        grid=(DEPTH, 3),
        in_specs=[
            pl.BlockSpec((MOD_ROWS, D_MODEL), lambda l, j: (0, 0)),
            pl.BlockSpec((1, D_MODEL, D_MODEL), lambda l, j: (l, 0, j)),
            pl.BlockSpec((1, 1, D_MODEL), lambda l, j: (l, 0, j)),
        ],
        out_specs=pl.BlockSpec((1, MOD_ROWS, D_MODEL), lambda l, j: (l, 0, j)),
        out_shape=jax.ShapeDtypeStruct((DEPTH, MOD_ROWS, 3 * D_MODEL), F32),
        compiler_params=pltpu.CompilerParams(
            dimension_semantics=("arbitrary", "arbitrary"), vmem_limit_bytes=VMEM_LIMIT_DEFAULT),
        name="modulation",
    )(c_all, w_mod, b_mod.reshape(DEPTH, 1, 3 * D_MODEL))


def _norm_rope(x, gain, cos, sin):
    lane = lax.broadcasted_iota(jnp.int32, x.shape, 1)
    first = lane < HEAD_DIM
    sq = x * x
    s_lo = jnp.sum(jnp.where(first, sq, 0.0), axis=-1, keepdims=True)
    s_hi = jnp.sum(jnp.where(first, 0.0, sq), axis=-1, keepdims=True)
    ms = jnp.where(first, s_lo, s_hi) * (1.0 / HEAD_DIM)
    xn = x * lax.rsqrt(ms + RMS_EPS) * gain
    quarter = HEAD_DIM // 4
    up = pltpu.roll(xn, LANES - quarter, 1)
    down = pltpu.roll(xn, quarter, 1)
    partner = jnp.where((lane & (2 * quarter - 1)) < quarter, up, down)
    return xn * cos + partner * sin


def _modulate(h_ref, mod_ref):
    shift = mod_ref[0, :, 0:D_MODEL]
    scale = mod_ref[0, :, D_MODEL:2 * D_MODEL]
    return (h_ref[0] * (1.0 + scale) + shift).astype(BF16)


def _store_kv_slots(k, v, k4_ref, v4_ref):
    first = lax.broadcasted_iota(jnp.int32, k.shape, 1) < HEAD_DIM
    ones = jnp.ones(k.shape, BF16)
    for t, ref in ((k, k4_ref), (v, v4_ref)):
        swapped = pltpu.roll(t, HEAD_DIM, 1)
        slots = (jnp.where(first, t, 0.0), jnp.where(first, 0.0, swapped),
                 jnp.where(first, swapped, 0.0), jnp.where(first, 0.0, t))
        for n, slot in enumerate(slots):
            ref[0, n, :, 0:LANES] = slot.astype(BF16)
    for n in range(N_SLOTS):
        v4_ref[0, n, :, LANES:V_SLOT_W] = ones


def _project_kernel(h_ref, mod_ref, w_ref, tab_ref, qg_ref, kg_ref,
                    q_ref, k4_ref, v4_ref, za_ref, yc_ref, bzc_ref, gm_ref):
    u = _modulate(h_ref, mod_ref)
    cos = tab_ref[:, 0:LANES]
    sin = tab_ref[:, LANES:2 * LANES]

    def proj(lo, hi):
        return jnp.dot(u, w_ref[:, lo:hi], preferred_element_type=F32)

    def named(name):
        return proj(*_OFF[name])

    x = named("q")
    for j in range(ATTN_W // LANES):
        q = _norm_rope(x[:, j * LANES:(j + 1) * LANES], qg_ref[...], cos, sin) * Q_SCALE
        q_ref[0, :, j * LANES:(j + 1) * LANES] = q.astype(BF16)
    kv = proj(_OFF["k"][0], _OFF["v"][1])
    k = _norm_rope(kv[:, 0:KV_W], kg_ref[...], cos, sin)
    _store_kv_slots(k, kv[:, KV_W:2 * KV_W], k4_ref, v4_ref)
    za_ref[0] = _silu(named("z_a")).astype(BF16)
    yc_ref[0, :, 0:SHORT_W] = (named("s_c") * named("s_h")).astype(BF16)
    yc_ref[0, :, SHORT_W:SHORT_W + CONF_W] = (named("c_a") * _sigmoid(named("c_g"))).astype(BF16)
    bzc_ref[0, :, 0:SHORT_W] = (named("s_b") * _silu(named("z_s"))).astype(BF16)
    bzc_ref[0, :, SHORT_W:SHORT_W + CONF_W] = _silu(named("z_c")).astype(BF16)
    g_lo = _OFF["g_m"][0]
    for j in range(N_BRANCH):
        gm_ref[0, :, j * D_MODEL:(j + 1) * D_MODEL] = _sigmoid(
            proj(g_lo + j * D_MODEL, g_lo + (j + 1) * D_MODEL)).astype(BF16)


def _project_kv_kernel(h_ref, mod_ref, w_ref, tab_ref, kg_ref, k4_ref, v4_ref):
    u = _modulate(h_ref, mod_ref)
    cos = tab_ref[:, 0:LANES]
    sin = tab_ref[:, LANES:2 * LANES]
    kv = jnp.dot(u, w_ref[...], preferred_element_type=F32)
    k = _norm_rope(kv[:, 0:KV_W], kg_ref[...], cos, sin)
    _store_kv_slots(k, kv[:, KV_W:2 * KV_W], k4_ref, v4_ref)


def _slot_spec(tm, width):
    return pl.BlockSpec((1, N_SLOTS, tm, width), lambda b, i: (b, 0, i, 0))


def _project(h, mod, w_bf, tabs, q_gain, k_gain, tm):
    n_batch, seq, _ = h.shape
    tok = lambda cols: pl.BlockSpec((1, tm, cols), lambda b, i: (b, i, 0))
    flat = lambda cols: jax.ShapeDtypeStruct((n_batch, seq, cols), BF16)
    slots = lambda width: jax.ShapeDtypeStruct((n_batch, N_SLOTS, seq, width), BF16)
    return pl.pallas_call(
        _project_kernel,
        grid=(n_batch, seq // tm),
        in_specs=[
            tok(D_MODEL),
            pl.BlockSpec((1, 1, 3 * D_MODEL), lambda b, i: (b, 0, 0)),
            pl.BlockSpec((D_MODEL, D_IN), lambda b, i: (0, 0), pipeline_mode=pl.Buffered(1)),
            pl.BlockSpec((tm, 2 * LANES), lambda b, i: (i, 0)),
            pl.BlockSpec((1, LANES), lambda b, i: (0, 0)),
            pl.BlockSpec((1, LANES), lambda b, i: (0, 0)),
        ],
        out_specs=[tok(ATTN_W), _slot_spec(tm, LANES), _slot_spec(tm, V_SLOT_W), tok(ATTN_W),
                   tok(SHORT_W + CONF_W), tok(SHORT_W + CONF_W), tok(N_BRANCH * D_MODEL)],
        out_shape=[flat(ATTN_W), slots(LANES), slots(V_SLOT_W), flat(ATTN_W),
                   flat(SHORT_W + CONF_W), flat(SHORT_W + CONF_W), flat(N_BRANCH * D_MODEL)],
        compiler_params=pltpu.CompilerParams(
            dimension_semantics=("arbitrary", "arbitrary"), vmem_limit_bytes=VMEM_LIMIT_PROJECT),
        name="project",
    )(h, mod, w_bf, tabs, q_gain, k_gain)


def _project_kv(h, mod, w_kv_bf, tabs, k_gain, tm):
    n_batch, seq, _ = h.shape
    slots = lambda width: jax.ShapeDtypeStruct((n_batch, N_SLOTS, seq, width), BF16)
    return pl.pallas_call(
        _project_kv_kernel,
        grid=(n_batch, seq // tm),
        in_specs=[
            pl.BlockSpec((1, tm, D_MODEL), lambda b, i: (b, i, 0)),
            pl.BlockSpec((1, 1, 3 * D_MODEL), lambda b, i: (b, 0, 0)),
            pl.BlockSpec((D_MODEL, 2 * KV_W), lambda b, i: (0, 0)),
            pl.BlockSpec((tm, 2 * LANES), lambda b, i: (i, 0)),
            pl.BlockSpec((1, LANES), lambda b, i: (0, 0)),
        ],
        out_specs=[_slot_spec(tm, LANES), _slot_spec(tm, V_SLOT_W)],
        out_shape=[slots(LANES), slots(V_SLOT_W)],
        compiler_params=pltpu.CompilerParams(
            dimension_semantics=("arbitrary", "arbitrary"), vmem_limit_bytes=VMEM_LIMIT_DEFAULT),
        name="project_kv",
    )(h, mod, w_kv_bf, tabs, k_gain)


def _attend_kernel(n_sets, tq, rows, q_ref, *refs):
    k_refs = refs[0:2 * n_sets:2]
    v_refs = refs[1:2 * n_sets:2]
    za_ref, o_ref = refs[2 * n_sets:]
    for r0 in range(0, tq, rows):
        for j in range(ATTN_W // LANES):
            kv = (2 * j) // GROUP
            qp = q_ref[0, r0:r0 + rows, j * LANES:(j + 1) * LANES]
            acc = None
            for half in range(2):
                slot = 2 * kv + half
                scores = [lax.dot_general(qp, k_ref[0, slot], (((1,), (1,)), ((), ())),
                                          preferred_element_type=F32) for k_ref in k_refs]
                m = None
                for s in scores:
                    ms = jnp.max(s, axis=-1, keepdims=True)
                    m = ms if m is None else jnp.maximum(m, ms)
                o = None
                for s, v_ref in zip(scores, v_refs):
                    part = jnp.dot(jnp.exp2(s - m).astype(BF16), v_ref[0, slot],
                                   preferred_element_type=F32)
                    o = part if o is None else o + part
                o = o[:, 0:LANES] * (1.0 / o[:, LANES:V_SLOT_W])
                acc = o if acc is None else acc + o
            za = za_ref[0, r0:r0 + rows, j * LANES:(j + 1) * LANES].astype(F32)
            o_ref[0, r0:r0 + rows, j * LANES:(j + 1) * LANES] = (acc * za).astype(BF16)


def _attend(q, kv_sets, za, tq, rows):
    n_batch, seq, _ = q.shape
    tok = pl.BlockSpec((1, tq, ATTN_W), lambda b, i: (b, i, 0))
    specs, operands = [], []
    for k4, v4 in kv_sets:
        n_keys = k4.shape[2]
        specs.append(pl.BlockSpec((1, N_SLOTS, n_keys, LANES), lambda b, i: (b, 0, 0, 0)))
        specs.append(pl.BlockSpec((1, N_SLOTS, n_keys, V_SLOT_W), lambda b, i: (b, 0, 0, 0)))
        operands += [k4, v4]
    return pl.pallas_call(
        functools.partial(_attend_kernel, len(kv_sets), tq, rows),
        grid=(n_batch, seq // tq),
        in_specs=[tok] + specs + [tok],
        out_specs=tok,
        out_shape=jax.ShapeDtypeStruct((n_batch, seq, ATTN_W), BF16),
        compiler_params=pltpu.CompilerParams(
            dimension_semantics=("arbitrary", "arbitrary"), vmem_limit_bytes=VMEM_LIMIT_DEFAULT),
        name="attend",
    )(q, *operands, za)


def _layer_norm(x, g, b):
    mu = jnp.mean(x, axis=-1, keepdims=True)
    xc = x - mu
    var = jnp.mean(xc * xc, axis=-1, keepdims=True)
    return xc * lax.rsqrt(var + LN_EPS) * g + b


def _merge_kernel(tile, sub, row_chunk, alpha,
                  ga_ref, yc_ref, ycp_ref, ycn_ref, bzc_ref, gm_ref, h_ref, mod_ref,
                  wpa_ref, wps_ref, wpc_ref, wo_ref, wsc_ref, wcdw_ref, bcdw_ref,
                  clng_ref, clnb_ref, bpc_ref, plg_ref, plb_ref,
                  o_ref, xbuf, xs, act_b, act_c):
    i = pl.program_id(1)
    n_tiles = pl.num_programs(1)
    conv_w = SHORT_W + CONF_W
    groups = row_chunk // SUBLANES

    xbuf[0:HALO] = ycp_ref[0].astype(F32)
    xbuf[HALO:HALO + tile] = yc_ref[0].astype(F32)
    xbuf[HALO + tile:HALO + tile + HALO] = ycn_ref[0].astype(F32)

    @pl.when(i == 0)
    def _():
        xbuf[0:HALO] = jnp.zeros((HALO, conv_w), F32)

    @pl.when(i == n_tiles - 1)
    def _():
        xbuf[HALO + tile:HALO + tile + HALO] = jnp.zeros((HALO, conv_w), F32)

    xs_groups = xs.shape[1]
    hc_in = xbuf[:, SHORT_W:conv_w]
    hc3 = hc_in.reshape(hc_in.shape[0] // SUBLANES, SUBLANES, CONF_W)
    sublane = lax.broadcasted_iota(jnp.int32, (1, SUBLANES, CONF_W), 1)
    xs[0] = hc3[0:xs_groups]
    rot = hc3
    for r in range(1, SUBLANES):
        rot = pltpu.roll(rot, SUBLANES - 1, 1)
        xs[r] = jnp.where(sublane < SUBLANES - r, rot[0:xs_groups], rot[1:xs_groups + 1])

    gate = mod_ref[0, :, 2 * D_MODEL:3 * D_MODEL]
    for s0 in range(0, tile, sub):
        for r0 in range(s0, s0 + sub, row_chunk):
            acc = None
            for k in range(SHORT_K):
                start = HALO + r0 + k - SHORT_K // 2
                term = xbuf[start:start + row_chunk, 0:SHORT_W] * wsc_ref[k:k + 1, :]
                acc = term if acc is None else acc + term
            bz = bzc_ref[0, r0:r0 + row_chunk, 0:SHORT_W].astype(F32)
            act_b[r0:r0 + row_chunk] = (acc * bz).astype(BF16)

            acc = None
            for k in range(CONF_K):
                start = HALO + r0 + k - CONF_K // 2
                r = start % SUBLANES
                g0 = (start - r) // SUBLANES
                term = xs[r, g0:g0 + groups] * wcdw_ref[k]
                acc = term if acc is None else acc + term
            conv = acc.reshape(row_chunk, CONF_W) + bcdw_ref[...]
            hc = _silu(_layer_norm(conv, clng_ref[...], clnb_ref[...]))
            zc = bzc_ref[0, r0:r0 + row_chunk, SHORT_W:conv_w].astype(F32)
            act_c[r0:r0 + row_chunk] = (hc * zc).astype(BF16)

        rows = slice(s0, s0 + sub)
        br_a = jnp.dot(ga_ref[0, rows, :], wpa_ref[...], preferred_element_type=F32)
        br_b = jnp.dot(act_b[rows, :], wps_ref[...], preferred_element_type=F32)
        br_c = jnp.dot(act_c[rows, :], wpc_ref[...], preferred_element_type=F32) + bpc_ref[...]
        merged = (gm_ref[0, rows, 0:D_MODEL] * br_a.astype(BF16)
                  + gm_ref[0, rows, D_MODEL:2 * D_MODEL] * br_b.astype(BF16)
                  + gm_ref[0, rows, 2 * D_MODEL:3 * D_MODEL] * br_c.astype(BF16))
        out = jnp.dot(merged, wo_ref[...], preferred_element_type=F32)
        o_ref[0, rows, :] = _layer_norm(alpha * h_ref[0, rows, :] + gate * out,
                                        plg_ref[...], plb_ref[...])


def _merge(gated, yc, bzc, gm, h, mod, w, tile, sub, row_chunk):
    n_batch, seq, _ = h.shape
    halo_blocks = tile // HALO
    n_halo = seq // HALO
    tok = lambda cols: pl.BlockSpec((1, tile, cols), lambda b, i: (b, i, 0))
    const = lambda shape: pl.BlockSpec(shape, lambda b, i: (0,) * len(shape))
    conv_w = SHORT_W + CONF_W
    alpha = (2.0 * DEPTH) ** 0.25
    xs_rows = tile + HALO + SUBLANES
    return pl.pallas_call(
        functools.partial(_merge_kernel, tile, sub, row_chunk, alpha),
        grid=(n_batch, seq // tile),
        in_specs=[
            tok(ATTN_W),
            tok(conv_w),
            pl.BlockSpec((1, HALO, conv_w), lambda b, i: (b, jnp.maximum(i * halo_blocks - 1, 0), 0)),
            pl.BlockSpec((1, HALO, conv_w),
                         lambda b, i: (b, jnp.minimum((i + 1) * halo_blocks, n_halo - 1), 0)),
            tok(conv_w),
            tok(N_BRANCH * D_MODEL),
            tok(D_MODEL),
            pl.BlockSpec((1, 1, 3 * D_MODEL), lambda b, i: (b, 0, 0)),
            const((ATTN_W, D_MODEL)), const((SHORT_W, D_MODEL)), const((CONF_W, D_MODEL)),
            const((D_MODEL, D_MODEL)),
            const((SHORT_K, SHORT_W)), const((CONF_K, SUBLANES, CONF_W)), const((1, CONF_W)),
            const((1, CONF_W)), const((1, CONF_W)), const((1, D_MODEL)),
            const((1, D_MODEL)), const((1, D_MODEL)),
        ],
        out_specs=tok(D_MODEL),
        out_shape=jax.ShapeDtypeStruct((n_batch, seq, D_MODEL), F32),
        scratch_shapes=[
            pltpu.VMEM((tile + 2 * HALO, conv_w), F32),
            pltpu.VMEM((SUBLANES, xs_rows // SUBLANES, SUBLANES, CONF_W), F32),
            pltpu.VMEM((tile, SHORT_W), BF16),
            pltpu.VMEM((tile, CONF_W), BF16),
        ],
        compiler_params=pltpu.CompilerParams(
            dimension_semantics=("arbitrary", "arbitrary"), vmem_limit_bytes=VMEM_LIMIT_DEFAULT),
        name="merge",
    )(gated, yc, yc, yc, bzc, gm, h, mod, *w)


def _rope_tables(n_tokens):
    n_rows = n_tokens // GRID_W
    rows = jnp.repeat(jnp.arange(n_rows, dtype=F32), GRID_W)
    cols = jnp.tile(jnp.arange(GRID_W, dtype=F32), n_rows)
    quarter = HEAD_DIM // 4
    freqs = ROPE_THETA ** (-jnp.arange(quarter, dtype=F32) / quarter)
    ang_r = rows[:, None] * freqs
    ang_c = cols[:, None] * freqs
    cos = jnp.concatenate([jnp.cos(ang_r), jnp.cos(ang_r), jnp.cos(ang_c), jnp.cos(ang_c)], axis=-1)
    sin = jnp.concatenate([-jnp.sin(ang_r), jnp.sin(ang_r), -jnp.sin(ang_c), jnp.sin(ang_c)], axis=-1)
    return jnp.concatenate([cos, cos, sin, sin], axis=-1)


def _identity_tables(n_tokens):
    return jnp.concatenate([jnp.ones((n_tokens, LANES), F32), jnp.zeros((n_tokens, LANES), F32)], axis=-1)


def kernel(x, c, ctx, c_ctx, w_mod, b_mod, w_in, q_gain, k_gain, w_proj_attn, w_short_conv, w_proj_short, w_conf_dw, b_conf_dw, conf_ln_g, conf_ln_b, w_proj_conf, b_proj_conf, w_out, post_ln_g, post_ln_b):
    n_batch, seq, _ = x.shape
    n_ctx = ctx.shape[1]
    lat_tile = 512
    ctx_tile = n_ctx
    attn_tile = 1024
    attn_rows = 256
    merge_tile = 512
    merge_sub = 256
    row_chunk = 64

    tabs_lat = _rope_tables(seq)
    tabs_ctx = _identity_tables(n_ctx)

    c_all = jnp.concatenate(
        [c, c_ctx[None, :], jnp.zeros((MOD_ROWS - n_batch - 1, D_MODEL), F32)], axis=0)
    mod_all = _modulation(c_all, w_mod, b_mod)

    h_lat, h_ctx = x, ctx
    for l in range(DEPTH):
        last = l == DEPTH - 1
        mod_lat = mod_all[l, :n_batch][:, None, :]
        mod_ctx = jnp.broadcast_to(mod_all[l, n_batch][None, None, :], (n_batch, 1, 3 * D_MODEL))
        w_bf = w_in[l].astype(BF16)
        qg = jnp.tile(q_gain[l], 2)[None, :]
        kg = jnp.tile(k_gain[l], 2)[None, :]
        branch_w = (w_proj_attn[l].astype(BF16), w_proj_short[l].astype(BF16),
                    w_proj_conf[l].astype(BF16), w_out[l].astype(BF16),
                    w_short_conv[l],
                    jnp.broadcast_to(w_conf_dw[l][:, None, :], (CONF_K, SUBLANES, CONF_W)),
                    b_conf_dw[l][None, :],
                    conf_ln_g[l][None, :], conf_ln_b[l][None, :], b_proj_conf[l][None, :],
                    post_ln_g[l][None, :], post_ln_b[l][None, :])

        if last:
            k4_c, v4_c = _project_kv(h_ctx, mod_ctx, w_bf[:, _OFF["k"][0]:_OFF["v"][1]],
                                     tabs_ctx, kg, ctx_tile)
        else:
            q_c, k4_c, v4_c, za_c, yc_c, bzc_c, gm_c = _project(
                h_ctx, mod_ctx, w_bf, tabs_ctx, qg, kg, ctx_tile)

        q_l, k4_l, v4_l, za_l, yc_l, bzc_l, gm_l = _project(
            h_lat, mod_lat, w_bf, tabs_lat, qg, kg, lat_tile)
        gated_l = _attend(q_l, ((k4_c, v4_c), (k4_l, v4_l)), za_l, attn_tile, attn_rows)
        h_lat_new = _merge(gated_l, yc_l, bzc_l, gm_l, h_lat, mod_lat, branch_w,
                           merge_tile, merge_sub, row_chunk)

        if not last:
            gated_c = _attend(q_c, ((k4_c, v4_c),), za_c, ctx_tile, ctx_tile)
            h_ctx = _merge(gated_c, yc_c, bzc_c, gm_c, h_ctx, mod_ctx, branch_w,
                           ctx_tile, ctx_tile, row_chunk)
        h_lat = h_lat_new
    return h_lat
```

```python
import functools
import math

import jax
import jax.numpy as jnp
from jax import lax
from jax.experimental import pallas as pl
from jax.experimental.pallas import tpu as pltpu

D_MODEL = 1024
DEPTH = 2
GRID_W = 64
N_HEADS = 8
N_KV_HEADS = 2
HEAD_DIM = 64
GROUP = N_HEADS // N_KV_HEADS
ATTN_W = N_HEADS * HEAD_DIM
KV_W = N_KV_HEADS * HEAD_DIM
SHORT_W = 512
SHORT_K = 3
CONF_W = 512
CONF_K = 31
N_BRANCH = 3
ROPE_THETA = 10000.0
RMS_EPS = 1e-6
LN_EPS = 1e-5
D_IN = 2 * ATTN_W + 2 * KV_W + 4 * SHORT_W + 3 * CONF_W + N_BRANCH * D_MODEL

_OFF = {}
_o = 0
for _name, _w in (("q", ATTN_W), ("k", KV_W), ("v", KV_W), ("z_a", ATTN_W), ("s_b", SHORT_W),
                  ("s_c", SHORT_W), ("s_h", SHORT_W), ("z_s", SHORT_W), ("c_a", CONF_W),
                  ("c_g", CONF_W), ("z_c", CONF_W), ("g_m", N_BRANCH * D_MODEL)):
    _OFF[_name] = (_o, _o + _w)
    _o += _w

LANES = 128
SUBLANES = 8
HALO = 16
MOD_ROWS = 24
N_SLOTS = 2 * N_KV_HEADS
V_SLOT_W = 2 * LANES
VMEM_LIMIT_PROJECT = 56 * 1024 * 1024
VMEM_LIMIT_DEFAULT = 48 * 1024 * 1024
Q_SCALE = (HEAD_DIM ** -0.5) * math.log2(math.e)

F32 = jnp.float32
BF16 = jnp.bfloat16


def _sigmoid(x):
    return 1.0 / (1.0 + jnp.exp(-x))


def _silu(x):
    return x * _sigmoid(x)


def _mod_kernel(c_ref, w_ref, b_ref, o_ref):
    a = _silu(c_ref[...])
    o_ref[0] = jnp.dot(a, w_ref[0], precision=lax.Precision.HIGHEST,
                       preferred_element_type=F32) + b_ref[0]


def _modulation(c_all, w_mod, b_mod):
    return pl.pallas_call(
        _mod_kernel,
        grid=(DEPTH, 3),
        in_specs=[
            pl.BlockSpec((MOD_ROWS, D_MODEL), lambda l, j: (0, 0)),
            pl.BlockSpec((1, D_MODEL, D_MODEL), lambda l, j: (l, 0, j)),
            pl.BlockSpec((1, 1, D_MODEL), lambda l, j: (l, 0, j)),
        ],
        out_specs=pl.BlockSpec((1, MOD_ROWS, D_MODEL), lambda l, j: (l, 0, j)),
        out_shape=jax.ShapeDtypeStruct((DEPTH, MOD_ROWS, 3 * D_MODEL), F32),
        compiler_params=pltpu.CompilerParams(
            dimension_semantics=("arbitrary", "arbitrary"), vmem_limit_bytes=VMEM_LIMIT_DEFAULT),
        name="modulation",
    )(c_all, w_mod, b_mod.reshape(DEPTH, 1, 3 * D_MODEL))


def _norm_rope(x, gain, cos, sin):
    lane = lax.broadcasted_iota(jnp.int32, x.shape, 1)
    first = lane < HEAD_DIM
    sq = x * x
    s_lo = jnp.sum(jnp.where(first, sq, 0.0), axis=-1, keepdims=True)
    s_hi = jnp.sum(jnp.where(first, 0.0, sq), axis=-1, keepdims=True)
    ms = jnp.where(first, s_lo, s_hi) * (1.0 / HEAD_DIM)
    xn = x * lax.rsqrt(ms + RMS_EPS) * gain
    quarter = HEAD_DIM // 4
    up = pltpu.roll(xn, LANES - quarter, 1)
    down = pltpu.roll(xn, quarter, 1)
    partner = jnp.where((lane & (2 * quarter - 1)) < quarter, up, down)
    return xn * cos + partner * sin


def _modulate(h_ref, mod_ref, rows):
    shift = mod_ref[0, :, 0:D_MODEL]
    scale = mod_ref[0, :, D_MODEL:2 * D_MODEL]
    return (h_ref[0, rows, :] * (1.0 + scale) + shift).astype(BF16)


def _store_kv_slots(k, v, k4_ref, v4_ref, rows):
    first = lax.broadcasted_iota(jnp.int32, k.shape, 1) < HEAD_DIM
    ones = jnp.ones(k.shape, BF16)
    for t, ref in ((k, k4_ref), (v, v4_ref)):
        swapped = pltpu.roll(t, HEAD_DIM, 1)
        slots = (jnp.where(first, t, 0.0), jnp.where(first, 0.0, swapped),
                 jnp.where(first, swapped, 0.0), jnp.where(first, 0.0, t))
        for n, slot in enumerate(slots):
            ref[0, n, rows, 0:LANES] = slot.astype(BF16)
    for n in range(N_SLOTS):
        v4_ref[0, n, rows, LANES:V_SLOT_W] = ones


def _project_kernel(tile, sub, h_ref, mod_ref, w_ref, tab_ref, qg_ref, kg_ref,
                    q_ref, k4_ref, v4_ref, za_ref, yc_ref, bzc_ref, gm_ref):
    for s0 in range(0, tile, sub):
        rows = slice(s0, s0 + sub)
        u = _modulate(h_ref, mod_ref, rows)
        cos = tab_ref[rows, 0:LANES]
        sin = tab_ref[rows, LANES:2 * LANES]

        def proj(lo, hi, u=u):
            return jnp.dot(u, w_ref[:, lo:hi], preferred_element_type=F32)

        def named(name, proj=proj):
            return proj(*_OFF[name])

        x = named("q")
        for j in range(ATTN_W // LANES):
            q = _norm_rope(x[:, j * LANES:(j + 1) * LANES], qg_ref[...], cos, sin) * Q_SCALE
            q_ref[0, rows, j * LANES:(j + 1) * LANES] = q.astype(BF16)
        kv = proj(_OFF["k"][0], _OFF["v"][1])
        k = _norm_rope(kv[:, 0:KV_W], kg_ref[...], cos, sin)
        _store_kv_slots(k, kv[:, KV_W:2 * KV_W], k4_ref, v4_ref, rows)
        za_ref[0, rows, :] = _silu(named("z_a")).astype(BF16)
        yc_ref[0, rows, 0:SHORT_W] = (named("s_c") * named("s_h")).astype(BF16)
        yc_ref[0, rows, SHORT_W:SHORT_W + CONF_W] = (
            named("c_a") * _sigmoid(named("c_g"))).astype(BF16)
        bzc_ref[0, rows, 0:SHORT_W] = (named("s_b") * _silu(named("z_s"))).astype(BF16)
        bzc_ref[0, rows, SHORT_W:SHORT_W + CONF_W] = _silu(named("z_c")).astype(BF16)
        g_lo = _OFF["g_m"][0]
        for j in range(N_BRANCH):
            gm_ref[0, rows, j * D_MODEL:(j + 1) * D_MODEL] = _sigmoid(
                proj(g_lo + j * D_MODEL, g_lo + (j + 1) * D_MODEL)).astype(BF16)


def _project_kv_kernel(h_ref, mod_ref, w_ref, tab_ref, kg_ref, k4_ref, v4_ref):
    rows = slice(None)
    u = _modulate(h_ref, mod_ref, rows)
    cos = tab_ref[:, 0:LANES]
    sin = tab_ref[:, LANES:2 * LANES]
    kv = jnp.dot(u, w_ref[...], preferred_element_type=F32)
    k = _norm_rope(kv[:, 0:KV_W], kg_ref[...], cos, sin)
    _store_kv_slots(k, kv[:, KV_W:2 * KV_W], k4_ref, v4_ref, rows)


def _slot_spec(tm, width):
    return pl.BlockSpec((1, N_SLOTS, tm, width), lambda b, i: (b, 0, i, 0))


def _project(h, mod, w_bf, tabs, q_gain, k_gain, tm, sub):
    n_batch, seq, _ = h.shape
    tok = lambda cols: pl.BlockSpec((1, tm, cols), lambda b, i: (b, i, 0))
    flat = lambda cols: jax.ShapeDtypeStruct((n_batch, seq, cols), BF16)
    slots = lambda width: jax.ShapeDtypeStruct((n_batch, N_SLOTS, seq, width), BF16)
    return pl.pallas_call(
        functools.partial(_project_kernel, tm, sub),
        grid=(n_batch, seq // tm),
        in_specs=[
            tok(D_MODEL),
            pl.BlockSpec((1, 1, 3 * D_MODEL), lambda b, i: (b, 0, 0)),
            pl.BlockSpec((D_MODEL, D_IN), lambda b, i: (0, 0), pipeline_mode=pl.Buffered(1)),
            pl.BlockSpec((tm, 2 * LANES), lambda b, i: (i, 0)),
            pl.BlockSpec((1, LANES), lambda b, i: (0, 0)),
            pl.BlockSpec((1, LANES), lambda b, i: (0, 0)),
        ],
        out_specs=[tok(ATTN_W), _slot_spec(tm, LANES), _slot_spec(tm, V_SLOT_W), tok(ATTN_W),
                   tok(SHORT_W + CONF_W), tok(SHORT_W + CONF_W), tok(N_BRANCH * D_MODEL)],
        out_shape=[flat(ATTN_W), slots(LANES), slots(V_SLOT_W), flat(ATTN_W),
                   flat(SHORT_W + CONF_W), flat(SHORT_W + CONF_W), flat(N_BRANCH * D_MODEL)],
        compiler_params=pltpu.CompilerParams(
            dimension_semantics=("arbitrary", "arbitrary"), vmem_limit_bytes=VMEM_LIMIT_PROJECT),
        name="project",
    )(h, mod, w_bf, tabs, q_gain, k_gain)


def _project_kv(h, mod, w_kv_bf, tabs, k_gain, tm):
    n_batch, seq, _ = h.shape
    slots = lambda width: jax.ShapeDtypeStruct((n_batch, N_SLOTS, seq, width), BF16)
    return pl.pallas_call(
        _project_kv_kernel,
        grid=(n_batch, seq // tm),
        in_specs=[
            pl.BlockSpec((1, tm, D_MODEL), lambda b, i: (b, i, 0)),
            pl.BlockSpec((1, 1, 3 * D_MODEL), lambda b, i: (b, 0, 0)),
            pl.BlockSpec((D_MODEL, 2 * KV_W), lambda b, i: (0, 0)),
            pl.BlockSpec((tm, 2 * LANES), lambda b, i: (i, 0)),
            pl.BlockSpec((1, LANES), lambda b, i: (0, 0)),
        ],
        out_specs=[_slot_spec(tm, LANES), _slot_spec(tm, V_SLOT_W)],
        out_shape=[slots(LANES), slots(V_SLOT_W)],
        compiler_params=pltpu.CompilerParams(
            dimension_semantics=("arbitrary", "arbitrary"), vmem_limit_bytes=VMEM_LIMIT_DEFAULT),
        name="project_kv",
    )(h, mod, w_kv_bf, tabs, k_gain)


def _attend_kernel(n_sets, tq, rows, q_ref, *refs):
    k_refs = refs[0:2 * n_sets:2]
    v_refs = refs[1:2 * n_sets:2]
    za_ref, o_ref = refs[2 * n_sets:]
    for r0 in range(0, tq, rows):
        for j in range(ATTN_W // LANES):
            kv = (2 * j) // GROUP
            qp = q_ref[0, r0:r0 + rows, j * LANES:(j + 1) * LANES]
            acc = None
            for half in range(2):
                slot = 2 * kv + half
                scores = [lax.dot_general(qp, k_ref[0, slot], (((1,), (1,)), ((), ())),
                                          preferred_element_type=F32) for k_ref in k_refs]
                m = None
                for s in scores:
                    ms = jnp.max(s, axis=-1, keepdims=True)
                    m = ms if m is None else jnp.maximum(m, ms)
                o = None
                for s, v_ref in zip(scores, v_refs):
                    part = jnp.dot(jnp.exp2(s - m).astype(BF16), v_ref[0, slot],
                                   preferred_element_type=F32)
                    o = part if o is None else o + part
                o = o[:, 0:LANES] * (1.0 / o[:, LANES:V_SLOT_W])
                acc = o if acc is None else acc + o
            za = za_ref[0, r0:r0 + rows, j * LANES:(j + 1) * LANES].astype(F32)
            o_ref[0, r0:r0 + rows, j * LANES:(j + 1) * LANES] = (acc * za).astype(BF16)


def _attend(q, kv_sets, za, tq, rows):
    n_batch, seq, _ = q.shape
    tok = pl.BlockSpec((1, tq, ATTN_W), lambda b, i: (b, i, 0))
    specs, operands = [], []
    for k4, v4 in kv_sets:
        n_keys = k4.shape[2]
        specs.append(pl.BlockSpec((1, N_SLOTS, n_keys, LANES), lambda b, i: (b, 0, 0, 0)))
        specs.append(pl.BlockSpec((1, N_SLOTS, n_keys, V_SLOT_W), lambda b, i: (b, 0, 0, 0)))
        operands += [k4, v4]
    return pl.pallas_call(
        functools.partial(_attend_kernel, len(kv_sets), tq, rows),
        grid=(n_batch, seq // tq),
        in_specs=[tok] + specs + [tok],
        out_specs=tok,
        out_shape=jax.ShapeDtypeStruct((n_batch, seq, ATTN_W), BF16),
        compiler_params=pltpu.CompilerParams(
            dimension_semantics=("arbitrary", "arbitrary"), vmem_limit_bytes=VMEM_LIMIT_DEFAULT),
        name="attend",
    )(q, *operands, za)


def _layer_norm(x, g, b):
    mu = jnp.mean(x, axis=-1, keepdims=True)
    xc = x - mu
    var = jnp.mean(xc * xc, axis=-1, keepdims=True)
    return xc * lax.rsqrt(var + LN_EPS) * g + b


def _merge_kernel(tile, sub, row_chunk, alpha,
                  ga_ref, yc_ref, ycp_ref, ycn_ref, bzc_ref, gm_ref, h_ref, mod_ref,
                  wpa_ref, wps_ref, wpc_ref, wo_ref, wsc_ref, wcdw_ref, bcdw_ref,
                  clng_ref, clnb_ref, bpc_ref, plg_ref, plb_ref,
                  o_ref, xbuf, xs, act_b, act_c):
    i = pl.program_id(1)
    n_tiles = pl.num_programs(1)
    conv_w = SHORT_W + CONF_W
    groups = row_chunk // SUBLANES

    xbuf[0:HALO] = ycp_ref[0].astype(F32)
    xbuf[HALO:HALO + tile] = yc_ref[0].astype(F32)
    xbuf[HALO + tile:HALO + tile + HALO] = ycn_ref[0].astype(F32)

    @pl.when(i == 0)
    def _():
        xbuf[0:HALO] = jnp.zeros((HALO, conv_w), F32)

    @pl.when(i == n_tiles - 1)
    def _():
        xbuf[HALO + tile:HALO + tile + HALO] = jnp.zeros((HALO, conv_w), F32)

    xs_groups = xs.shape[1]
    hc_in = xbuf[:, SHORT_W:conv_w]
    hc3 = hc_in.reshape(hc_in.shape[0] // SUBLANES, SUBLANES, CONF_W)
    sublane = lax.broadcasted_iota(jnp.int32, (1, SUBLANES, CONF_W), 1)
    xs[0] = hc3[0:xs_groups]
    rot = hc3
    for r in range(1, SUBLANES):
        rot = pltpu.roll(rot, SUBLANES - 1, 1)
        xs[r] = jnp.where(sublane < SUBLANES - r, rot[0:xs_groups], rot[1:xs_groups + 1])

    gate = mod_ref[0, :, 2 * D_MODEL:3 * D_MODEL]
    for s0 in range(0, tile, sub):
        for r0 in range(s0, s0 + sub, row_chunk):
            acc = None
            for k in range(SHORT_K):
                start = HALO + r0 + k - SHORT_K // 2
                term = xbuf[start:start + row_chunk, 0:SHORT_W] * wsc_ref[k:k + 1, :]
                acc = term if acc is None else acc + term
            bz = bzc_ref[0, r0:r0 + row_chunk, 0:SHORT_W].astype(F32)
            act_b[r0:r0 + row_chunk] = (acc * bz).astype(BF16)

            acc = None
            for k in range(CONF_K):
                start = HALO + r0 + k - CONF_K // 2
                r = start % SUBLANES
                g0 = (start - r) // SUBLANES
                term = xs[r, g0:g0 + groups] * wcdw_ref[k]
                acc = term if acc is None else acc + term
            conv = acc.reshape(row_chunk, CONF_W) + bcdw_ref[...]
            hc = _silu(_layer_norm(conv, clng_ref[...], clnb_ref[...]))
            zc = bzc_ref[0, r0:r0 + row_chunk, SHORT_W:conv_w].astype(F32)
            act_c[r0:r0 + row_chunk] = (hc * zc).astype(BF16)

        rows = slice(s0, s0 + sub)
        br_a = jnp.dot(ga_ref[0, rows, :], wpa_ref[...], preferred_element_type=F32)
        br_b = jnp.dot(act_b[rows, :], wps_ref[...], preferred_element_type=F32)
        br_c = jnp.dot(act_c[rows, :], wpc_ref[...], preferred_element_type=F32) + bpc_ref[...]
        merged = (gm_ref[0, rows, 0:D_MODEL] * br_a.astype(BF16)
                  + gm_ref[0, rows, D_MODEL:2 * D_MODEL] * br_b.astype(BF16)
                  + gm_ref[0, rows, 2 * D_MODEL:3 * D_MODEL] * br_c.astype(BF16))
        out = jnp.dot(merged, wo_ref[...], preferred_element_type=F32)
        o_ref[0, rows, :] = _layer_norm(alpha * h_ref[0, rows, :] + gate * out,
                                        plg_ref[...], plb_ref[...])


def _merge(gated, yc, bzc, gm, h, mod, w, tile, sub, row_chunk):
    n_batch, seq, _ = h.shape
    halo_blocks = tile // HALO
    n_halo = seq // HALO
    tok = lambda cols: pl.BlockSpec((1, tile, cols), lambda b, i: (b, i, 0))
    const = lambda shape: pl.BlockSpec(shape, lambda b, i: (0,) * len(shape))
    conv_w = SHORT_W + CONF_W
    alpha = (2.0 * DEPTH) ** 0.25
    xs_rows = tile + HALO + SUBLANES
    return pl.pallas_call(
        functools.partial(_merge_kernel, tile, sub, row_chunk, alpha),
        grid=(n_batch, seq // tile),
        in_specs=[
            tok(ATTN_W),
            tok(conv_w),
            pl.BlockSpec((1, HALO, conv_w), lambda b, i: (b, jnp.maximum(i * halo_blocks - 1, 0), 0)),
            pl.BlockSpec((1, HALO, conv_w),
                         lambda b, i: (b, jnp.minimum((i + 1) * halo_blocks, n_halo - 1), 0)),
            tok(conv_w),
            tok(N_BRANCH * D_MODEL),
            tok(D_MODEL),
            pl.BlockSpec((1, 1, 3 * D_MODEL), lambda b, i: (b, 0, 0)),
            const((ATTN_W, D_MODEL)), const((SHORT_W, D_MODEL)), const((CONF_W, D_MODEL)),
            const((D_MODEL, D_MODEL)),
            const((SHORT_K, SHORT_W)), const((CONF_K, SUBLANES, CONF_W)), const((1, CONF_W)),
            const((1, CONF_W)), const((1, CONF_W)), const((1, D_MODEL)),
            const((1, D_MODEL)), const((1, D_MODEL)),
        ],
        out_specs=tok(D_MODEL),
        out_shape=jax.ShapeDtypeStruct((n_batch, seq, D_MODEL), F32),
        scratch_shapes=[
            pltpu.VMEM((tile + 2 * HALO, conv_w), F32),
            pltpu.VMEM((SUBLANES, xs_rows // SUBLANES, SUBLANES, CONF_W), F32),
            pltpu.VMEM((tile, SHORT_W), BF16),
            pltpu.VMEM((tile, CONF_W), BF16),
        ],
        compiler_params=pltpu.CompilerParams(
            dimension_semantics=("arbitrary", "arbitrary"), vmem_limit_bytes=VMEM_LIMIT_DEFAULT),
        name="merge",
    )(gated, yc, yc, yc, bzc, gm, h, mod, *w)


def _rope_tables(n_tokens):
    n_rows = n_tokens // GRID_W
    rows = jnp.repeat(jnp.arange(n_rows, dtype=F32), GRID_W)
    cols = jnp.tile(jnp.arange(GRID_W, dtype=F32), n_rows)
    quarter = HEAD_DIM // 4
    freqs = ROPE_THETA ** (-jnp.arange(quarter, dtype=F32) / quarter)
    ang_r = rows[:, None] * freqs
    ang_c = cols[:, None] * freqs
    cos = jnp.concatenate([jnp.cos(ang_r), jnp.cos(ang_r), jnp.cos(ang_c), jnp.cos(ang_c)], axis=-1)
    sin = jnp.concatenate([-jnp.sin(ang_r), jnp.sin(ang_r), -jnp.sin(ang_c), jnp.sin(ang_c)], axis=-1)
    return jnp.concatenate([cos, cos, sin, sin], axis=-1)


def _identity_tables(n_tokens):
    return jnp.concatenate([jnp.ones((n_tokens, LANES), F32), jnp.zeros((n_tokens, LANES), F32)], axis=-1)


def kernel(x, c, ctx, c_ctx, w_mod, b_mod, w_in, q_gain, k_gain, w_proj_attn, w_short_conv, w_proj_short, w_conf_dw, b_conf_dw, conf_ln_g, conf_ln_b, w_proj_conf, b_proj_conf, w_out, post_ln_g, post_ln_b):
    n_batch, seq, _ = x.shape
    n_ctx = ctx.shape[1]
    lat_tile = 512
    lat_sub = 256
    ctx_tile = n_ctx
    attn_tile = 1024
    attn_rows = 256
    merge_tile = 512
    merge_sub = 256
    row_chunk = 64

    tabs_lat = _rope_tables(seq)
    tabs_ctx = _identity_tables(n_ctx)

    c_all = jnp.concatenate(
        [c, c_ctx[None, :], jnp.zeros((MOD_ROWS - n_batch - 1, D_MODEL), F32)], axis=0)
    mod_all = _modulation(c_all, w_mod, b_mod)

    h_lat, h_ctx = x, ctx
    for l in range(DEPTH):
        last = l == DEPTH - 1
        mod_lat = mod_all[l, :n_batch][:, None, :]
        mod_ctx = jnp.broadcast_to(mod_all[l, n_batch][None, None, :], (n_batch, 1, 3 * D_MODEL))
        w_bf = w_in[l].astype(BF16)
        qg = jnp.tile(q_gain[l], 2)[None, :]
        kg = jnp.tile(k_gain[l], 2)[None, :]
        branch_w = (w_proj_attn[l].astype(BF16), w_proj_short[l].astype(BF16),
                    w_proj_conf[l].astype(BF16), w_out[l].astype(BF16),
                    w_short_conv[l],
                    jnp.broadcast_to(w_conf_dw[l][:, None, :], (CONF_K, SUBLANES, CONF_W)),
                    b_conf_dw[l][None, :],
                    conf_ln_g[l][None, :], conf_ln_b[l][None, :], b_proj_conf[l][None, :],
                    post_ln_g[l][None, :], post_ln_b[l][None, :])

        if last:
            k4_c, v4_c = _project_kv(h_ctx, mod_ctx, w_bf[:, _OFF["k"][0]:_OFF["v"][1]],
                                     tabs_ctx, kg, ctx_tile)
        else:
            q_c, k4_c, v4_c, za_c, yc_c, bzc_c, gm_c = _project(
                h_ctx, mod_ctx, w_bf, tabs_ctx, qg, kg, ctx_tile, ctx_tile)

        q_l, k4_l, v4_l, za_l, yc_l, bzc_l, gm_l = _project(
            h_lat, mod_lat, w_bf, tabs_lat, qg, kg, lat_tile, lat_sub)
        gated_l = _attend(q_l, ((k4_c, v4_c), (k4_l, v4_l)), za_l, attn_tile, attn_rows)
        h_lat_new = _merge(gated_l, yc_l, bzc_l, gm_l, h_lat, mod_lat, branch_w,
                           merge_tile, merge_sub, row_chunk)

        if not last:
            gated_c = _attend(q_c, ((k4_c, v4_c),), za_c, ctx_tile, ctx_tile)
            h_ctx = _merge(gated_c, yc_c, bzc_c, gm_c, h_ctx, mod_ctx, branch_w,
                           ctx_tile, ctx_tile, row_chunk)
        h_lat = h_lat_new
    return h_lat
```

```python
import functools
import math

import jax
import jax.numpy as jnp
from jax import lax
from jax.experimental import pallas as pl
from jax.experimental.pallas import tpu as pltpu

D_MODEL = 1024
DEPTH = 2
GRID_W = 64
N_HEADS = 8
N_KV_HEADS = 2
HEAD_DIM = 64
GROUP = N_HEADS // N_KV_HEADS
ATTN_W = N_HEADS * HEAD_DIM
KV_W = N_KV_HEADS * HEAD_DIM
SHORT_W = 512
SHORT_K = 3
CONF_W = 512
CONF_K = 31
N_BRANCH = 3
ROPE_THETA = 10000.0
RMS_EPS = 1e-6
LN_EPS = 1e-5
D_IN = 2 * ATTN_W + 2 * KV_W + 4 * SHORT_W + 3 * CONF_W + N_BRANCH * D_MODEL

_OFF = {}
_o = 0
for _name, _w in (("q", ATTN_W), ("k", KV_W), ("v", KV_W), ("z_a", ATTN_W), ("s_b", SHORT_W),
                  ("s_c", SHORT_W), ("s_h", SHORT_W), ("z_s", SHORT_W), ("c_a", CONF_W),
                  ("c_g", CONF_W), ("z_c", CONF_W), ("g_m", N_BRANCH * D_MODEL)):
    _OFF[_name] = (_o, _o + _w)
    _o += _w

LANES = 128
SUBLANES = 8
HALO = 16
MOD_ROWS = 24
N_SLOTS = 2 * N_KV_HEADS
V_SLOT_W = 2 * LANES
VMEM_LIMIT_PROJECT = 56 * 1024 * 1024
VMEM_LIMIT_DEFAULT = 48 * 1024 * 1024
Q_SCALE = (HEAD_DIM ** -0.5) * math.log2(math.e)

F32 = jnp.float32
BF16 = jnp.bfloat16


def _sigmoid(x):
    return 1.0 / (1.0 + jnp.exp(-x))


def _silu(x):
    return x * _sigmoid(x)


def _mod_kernel(c_ref, w_ref, b_ref, o_ref):
    a = _silu(c_ref[...])
    o_ref[0] = jnp.dot(a, w_ref[0], precision=lax.Precision.HIGHEST,
                       preferred_element_type=F32) + b_ref[0]


def _modulation(c_all, w_mod, b_mod):
    return pl.pallas_call(
        _mod_kernel,
        grid=(DEPTH, 3),
        in_specs=[
            pl.BlockSpec((MOD_ROWS, D_MODEL), lambda l, j: (0, 0)),
            pl.BlockSpec((1, D_MODEL, D_MODEL), lambda l, j: (l, 0, j)),
            pl.BlockSpec((1, 1, D_MODEL), lambda l, j: (l, 0, j)),
        ],
        out_specs=pl.BlockSpec((1, MOD_ROWS, D_MODEL), lambda l, j: (l, 0, j)),
        out_shape=jax.ShapeDtypeStruct((DEPTH, MOD_ROWS, 3 * D_MODEL), F32),
        compiler_params=pltpu.CompilerParams(
            dimension_semantics=("arbitrary", "arbitrary"), vmem_limit_bytes=VMEM_LIMIT_DEFAULT),
        name="modulation",
    )(c_all, w_mod, b_mod.reshape(DEPTH, 1, 3 * D_MODEL))


def _norm_rope(x, gain, cos, sin):
    lane = lax.broadcasted_iota(jnp.int32, x.shape, 1)
    first = lane < HEAD_DIM
    sq = x * x
    s_lo = jnp.sum(jnp.where(first, sq, 0.0), axis=-1, keepdims=True)
    s_hi = jnp.sum(jnp.where(first, 0.0, sq), axis=-1, keepdims=True)
    ms = jnp.where(first, s_lo, s_hi) * (1.0 / HEAD_DIM)
    xn = x * lax.rsqrt(ms + RMS_EPS) * gain
    quarter = HEAD_DIM // 4
    up = pltpu.roll(xn, LANES - quarter, 1)
    down = pltpu.roll(xn, quarter, 1)
    partner = jnp.where((lane & (2 * quarter - 1)) < quarter, up, down)
    return xn * cos + partner * sin


def _modulate(h_ref, mod_ref, rows):
    shift = mod_ref[0, :, 0:D_MODEL]
    scale = mod_ref[0, :, D_MODEL:2 * D_MODEL]
    return (h_ref[0, rows, :] * (1.0 + scale) + shift).astype(BF16)


def _store_kv_slots(k, v, k4_ref, v4_ref, rows):
    first = lax.broadcasted_iota(jnp.int32, k.shape, 1) < HEAD_DIM
    ones = jnp.ones(k.shape, BF16)
    for t, ref in ((k, k4_ref), (v, v4_ref)):
        swapped = pltpu.roll(t, HEAD_DIM, 1)
        slots = (jnp.where(first, t, 0.0), jnp.where(first, 0.0, swapped),
                 jnp.where(first, swapped, 0.0), jnp.where(first, 0.0, t))
        for n, slot in enumerate(slots):
            ref[0, n, rows, 0:LANES] = slot.astype(BF16)
    for n in range(N_SLOTS):
        v4_ref[0, n, rows, LANES:V_SLOT_W] = ones


def _project_kernel(tile, sub, h_ref, mod_ref, w_ref, tab_ref, qg_ref, kg_ref,
                    q_ref, k4_ref, v4_ref, za_ref, yc_ref, bzc_ref, gm_ref):
    for s0 in range(0, tile, sub):
        rows = slice(s0, s0 + sub)
        u = _modulate(h_ref, mod_ref, rows)
        cos = tab_ref[rows, 0:LANES]
        sin = tab_ref[rows, LANES:2 * LANES]

        def proj(lo, hi, u=u):
            return jnp.dot(u, w_ref[:, lo:hi], preferred_element_type=F32)

        def named(name, proj=proj):
            return proj(*_OFF[name])

        x = named("q")
        for j in range(ATTN_W // LANES):
            q = _norm_rope(x[:, j * LANES:(j + 1) * LANES], qg_ref[...], cos, sin) * Q_SCALE
            q_ref[0, rows, j * LANES:(j + 1) * LANES] = q.astype(BF16)
        kv = proj(_OFF["k"][0], _OFF["v"][1])
        k = _norm_rope(kv[:, 0:KV_W], kg_ref[...], cos, sin)
        _store_kv_slots(k, kv[:, KV_W:2 * KV_W], k4_ref, v4_ref, rows)
        za_ref[0, rows, :] = _silu(named("z_a")).astype(BF16)
        yc_ref[0, rows, 0:SHORT_W] = (named("s_c") * named("s_h")).astype(BF16)
        yc_ref[0, rows, SHORT_W:SHORT_W + CONF_W] = (
            named("c_a") * _sigmoid(named("c_g"))).astype(BF16)
        bzc_ref[0, rows, 0:SHORT_W] = (named("s_b") * _silu(named("z_s"))).astype(BF16)
        bzc_ref[0, rows, SHORT_W:SHORT_W + CONF_W] = _silu(named("z_c")).astype(BF16)
        g_lo = _OFF["g_m"][0]
        for j in range(N_BRANCH):
            gm_ref[0, rows, j * D_MODEL:(j + 1) * D_MODEL] = _sigmoid(
                proj(g_lo + j * D_MODEL, g_lo + (j + 1) * D_MODEL)).astype(BF16)


def _project_kv_kernel(h_ref, mod_ref, w_ref, tab_ref, kg_ref, k4_ref, v4_ref):
    rows = slice(None)
    u = _modulate(h_ref, mod_ref, rows)
    cos = tab_ref[:, 0:LANES]
    sin = tab_ref[:, LANES:2 * LANES]
    kv = jnp.dot(u, w_ref[...], preferred_element_type=F32)
    k = _norm_rope(kv[:, 0:KV_W], kg_ref[...], cos, sin)
    _store_kv_slots(k, kv[:, KV_W:2 * KV_W], k4_ref, v4_ref, rows)


def _slot_spec(tm, width):
    return pl.BlockSpec((1, N_SLOTS, tm, width), lambda b, i: (b, 0, i, 0))


def _project(h, mod, w_bf, tabs, q_gain, k_gain, tm, sub):
    n_batch, seq, _ = h.shape
    tok = lambda cols: pl.BlockSpec((1, tm, cols), lambda b, i: (b, i, 0))
    flat = lambda cols: jax.ShapeDtypeStruct((n_batch, seq, cols), BF16)
    slots = lambda width: jax.ShapeDtypeStruct((n_batch, N_SLOTS, seq, width), BF16)
    return pl.pallas_call(
        functools.partial(_project_kernel, tm, sub),
        grid=(n_batch, seq // tm),
        in_specs=[
            tok(D_MODEL),
            pl.BlockSpec((1, 1, 3 * D_MODEL), lambda b, i: (b, 0, 0)),
            pl.BlockSpec((D_MODEL, D_IN), lambda b, i: (0, 0), pipeline_mode=pl.Buffered(1)),
            pl.BlockSpec((tm, 2 * LANES), lambda b, i: (i, 0)),
            pl.BlockSpec((1, LANES), lambda b, i: (0, 0)),
            pl.BlockSpec((1, LANES), lambda b, i: (0, 0)),
        ],
        out_specs=[tok(ATTN_W), _slot_spec(tm, LANES), _slot_spec(tm, V_SLOT_W), tok(ATTN_W),
                   tok(SHORT_W + CONF_W), tok(SHORT_W + CONF_W), tok(N_BRANCH * D_MODEL)],
        out_shape=[flat(ATTN_W), slots(LANES), slots(V_SLOT_W), flat(ATTN_W),
                   flat(SHORT_W + CONF_W), flat(SHORT_W + CONF_W), flat(N_BRANCH * D_MODEL)],
        compiler_params=pltpu.CompilerParams(
            dimension_semantics=("arbitrary", "arbitrary"), vmem_limit_bytes=VMEM_LIMIT_PROJECT),
        name="project",
    )(h, mod, w_bf, tabs, q_gain, k_gain)


def _project_kv(h, mod, w_kv_bf, tabs, k_gain, tm):
    n_batch, seq, _ = h.shape
    slots = lambda width: jax.ShapeDtypeStruct((n_batch, N_SLOTS, seq, width), BF16)
    return pl.pallas_call(
        _project_kv_kernel,
        grid=(n_batch, seq // tm),
        in_specs=[
            pl.BlockSpec((1, tm, D_MODEL), lambda b, i: (b, i, 0)),
            pl.BlockSpec((1, 1, 3 * D_MODEL), lambda b, i: (b, 0, 0)),
            pl.BlockSpec((D_MODEL, 2 * KV_W), lambda b, i: (0, 0)),
            pl.BlockSpec((tm, 2 * LANES), lambda b, i: (i, 0)),
            pl.BlockSpec((1, LANES), lambda b, i: (0, 0)),
        ],
        out_specs=[_slot_spec(tm, LANES), _slot_spec(tm, V_SLOT_W)],
        out_shape=[slots(LANES), slots(V_SLOT_W)],
        compiler_params=pltpu.CompilerParams(
            dimension_semantics=("arbitrary", "arbitrary"), vmem_limit_bytes=VMEM_LIMIT_DEFAULT),
        name="project_kv",
    )(h, mod, w_kv_bf, tabs, k_gain)


def _attend_kernel(n_sets, tq, rows, q_ref, *refs):
    k_refs = refs[0:2 * n_sets:2]
    v_refs = refs[1:2 * n_sets:2]
    za_ref, o_ref = refs[2 * n_sets:]
    for r0 in range(0, tq, rows):
        for j in range(ATTN_W // LANES):
            kv = (2 * j) // GROUP
            qp = q_ref[0, r0:r0 + rows, j * LANES:(j + 1) * LANES]
            acc = None
            for half in range(2):
                slot = 2 * kv + half
                scores = [lax.dot_general(qp, k_ref[0, slot], (((1,), (1,)), ((), ())),
                                          preferred_element_type=F32) for k_ref in k_refs]
                m = None
                for s in scores:
                    ms = jnp.max(s, axis=-1, keepdims=True)
                    m = ms if m is None else jnp.maximum(m, ms)
                o = None
                for s, v_ref in zip(scores, v_refs):
                    part = jnp.dot(jnp.exp2(s - m).astype(BF16), v_ref[0, slot],
                                   preferred_element_type=F32)
                    o = part if o is None else o + part
                o = o[:, 0:LANES] * (1.0 / o[:, LANES:V_SLOT_W])
                acc = o if acc is None else acc + o
            za = za_ref[0, r0:r0 + rows, j * LANES:(j + 1) * LANES].astype(F32)
            o_ref[0, r0:r0 + rows, j * LANES:(j + 1) * LANES] = (acc * za).astype(BF16)


def _attend(q, kv_sets, za, tq, rows):
    n_batch, seq, _ = q.shape
    tok = pl.BlockSpec((1, tq, ATTN_W), lambda b, i: (b, i, 0))
    specs, operands = [], []
    for k4, v4 in kv_sets:
        n_keys = k4.shape[2]
        specs.append(pl.BlockSpec((1, N_SLOTS, n_keys, LANES), lambda b, i: (b, 0, 0, 0)))
        specs.append(pl.BlockSpec((1, N_SLOTS, n_keys, V_SLOT_W), lambda b, i: (b, 0, 0, 0)))
        operands += [k4, v4]
    return pl.pallas_call(
        functools.partial(_attend_kernel, len(kv_sets), tq, rows),
        grid=(n_batch, seq // tq),
        in_specs=[tok] + specs + [tok],
        out_specs=tok,
        out_shape=jax.ShapeDtypeStruct((n_batch, seq, ATTN_W), BF16),
        compiler_params=pltpu.CompilerParams(
            dimension_semantics=("arbitrary", "arbitrary"), vmem_limit_bytes=VMEM_LIMIT_DEFAULT),
        name="attend",
    )(q, *operands, za)


def _layer_norm(x, g, b):
    mu = jnp.mean(x, axis=-1, keepdims=True)
    xc = x - mu
    var = jnp.mean(xc * xc, axis=-1, keepdims=True)
    return xc * lax.rsqrt(var + LN_EPS) * g + b


def _merge_kernel(tile, sub, row_chunk, alpha,
                  ga_ref, yc_ref, ycp_ref, ycn_ref, bzc_ref, gm_ref, h_ref, mod_ref,
                  wpa_ref, wps_ref, wpc_ref, wo_ref, wsc_ref, wcdw_ref, bcdw_ref,
                  clng_ref, clnb_ref, bpc_ref, plg_ref, plb_ref,
                  o_ref, xbuf, xs, act_b, act_c):
    i = pl.program_id(1)
    n_tiles = pl.num_programs(1)
    conv_w = SHORT_W + CONF_W
    groups = row_chunk // SUBLANES

    xbuf[0:HALO] = ycp_ref[0].astype(F32)
    xbuf[HALO:HALO + tile] = yc_ref[0].astype(F32)
    xbuf[HALO + tile:HALO + tile + HALO] = ycn_ref[0].astype(F32)

    @pl.when(i == 0)
    def _():
        xbuf[0:HALO] = jnp.zeros((HALO, conv_w), F32)

    @pl.when(i == n_tiles - 1)
    def _():
        xbuf[HALO + tile:HALO + tile + HALO] = jnp.zeros((HALO, conv_w), F32)

    xs_groups = xs.shape[1]
    hc_in = xbuf[:, SHORT_W:conv_w]
    hc3 = hc_in.reshape(hc_in.shape[0] // SUBLANES, SUBLANES, CONF_W)
    sublane = lax.broadcasted_iota(jnp.int32, (1, SUBLANES, CONF_W), 1)
    xs[0] = hc3[0:xs_groups]
    rot = hc3
    for r in range(1, SUBLANES):
        rot = pltpu.roll(rot, SUBLANES - 1, 1)
        xs[r] = jnp.where(sublane < SUBLANES - r, rot[0:xs_groups], rot[1:xs_groups + 1])

    gate = mod_ref[0, :, 2 * D_MODEL:3 * D_MODEL]
    for s0 in range(0, tile, sub):
        for r0 in range(s0, s0 + sub, row_chunk):
            acc = None
            for k in range(SHORT_K):
                start = HALO + r0 + k - SHORT_K // 2
                term = xbuf[start:start + row_chunk, 0:SHORT_W] * wsc_ref[k:k + 1, :]
                acc = term if acc is None else acc + term
            bz = bzc_ref[0, r0:r0 + row_chunk, 0:SHORT_W].astype(F32)
            act_b[r0:r0 + row_chunk] = (acc * bz).astype(BF16)

            acc = None
            for k in range(CONF_K):
                start = HALO + r0 + k - CONF_K // 2
                r = start % SUBLANES
                g0 = (start - r) // SUBLANES
                term = xs[r, g0:g0 + groups] * wcdw_ref[k]
                acc = term if acc is None else acc + term
            conv = acc.reshape(row_chunk, CONF_W) + bcdw_ref[...]
            hc = _silu(_layer_norm(conv, clng_ref[...], clnb_ref[...]))
            zc = bzc_ref[0, r0:r0 + row_chunk, SHORT_W:conv_w].astype(F32)
            act_c[r0:r0 + row_chunk] = (hc * zc).astype(BF16)

        rows = slice(s0, s0 + sub)
        br_a = jnp.dot(ga_ref[0, rows, :], wpa_ref[...], preferred_element_type=F32)
        br_b = jnp.dot(act_b[rows, :], wps_ref[...], preferred_element_type=F32)
        br_c = jnp.dot(act_c[rows, :], wpc_ref[...], preferred_element_type=F32) + bpc_ref[...]
        merged = (gm_ref[0, rows, 0:D_MODEL] * br_a.astype(BF16)
                  + gm_ref[0, rows, D_MODEL:2 * D_MODEL] * br_b.astype(BF16)
                  + gm_ref[0, rows, 2 * D_MODEL:3 * D_MODEL] * br_c.astype(BF16))
        out = jnp.dot(merged, wo_ref[...], preferred_element_type=F32)
        o_ref[0, rows, :] = _layer_norm(alpha * h_ref[0, rows, :] + gate * out,
                                        plg_ref[...], plb_ref[...])


def _merge(gated, yc, bzc, gm, h, mod, w, tile, sub, row_chunk):
    n_batch, seq, _ = h.shape
    halo_blocks = tile // HALO
    n_halo = seq // HALO
    tok = lambda cols: pl.BlockSpec((1, tile, cols), lambda b, i: (b, i, 0))
    const = lambda shape: pl.BlockSpec(shape, lambda b, i: (0,) * len(shape))
    conv_w = SHORT_W + CONF_W
    alpha = (2.0 * DEPTH) ** 0.25
    xs_rows = tile + HALO + SUBLANES
    return pl.pallas_call(
        functools.partial(_merge_kernel, tile, sub, row_chunk, alpha),
        grid=(n_batch, seq // tile),
        in_specs=[
            tok(ATTN_W),
            tok(conv_w),
            pl.BlockSpec((1, HALO, conv_w), lambda b, i: (b, jnp.maximum(i * halo_blocks - 1, 0), 0)),
            pl.BlockSpec((1, HALO, conv_w),
                         lambda b, i: (b, jnp.minimum((i + 1) * halo_blocks, n_halo - 1), 0)),
            tok(conv_w),
            tok(N_BRANCH * D_MODEL),
            tok(D_MODEL),
            pl.BlockSpec((1, 1, 3 * D_MODEL), lambda b, i: (b, 0, 0)),
            const((ATTN_W, D_MODEL)), const((SHORT_W, D_MODEL)), const((CONF_W, D_MODEL)),
            const((D_MODEL, D_MODEL)),
            const((SHORT_K, SHORT_W)), const((CONF_K, SUBLANES, CONF_W)), const((1, CONF_W)),
            const((1, CONF_W)), const((1, CONF_W)), const((1, D_MODEL)),
            const((1, D_MODEL)), const((1, D_MODEL)),
        ],
        out_specs=tok(D_MODEL),
        out_shape=jax.ShapeDtypeStruct((n_batch, seq, D_MODEL), F32),
        scratch_shapes=[
            pltpu.VMEM((tile + 2 * HALO, conv_w), F32),
            pltpu.VMEM((SUBLANES, xs_rows // SUBLANES, SUBLANES, CONF_W), F32),
            pltpu.VMEM((tile, SHORT_W), BF16),
            pltpu.VMEM((tile, CONF_W), BF16),
        ],
        compiler_params=pltpu.CompilerParams(
            dimension_semantics=("arbitrary", "arbitrary"), vmem_limit_bytes=VMEM_LIMIT_DEFAULT),
        name="merge",
    )(gated, yc, yc, yc, bzc, gm, h, mod, *w)


def _rope_tables(n_tokens):
    quarter = HEAD_DIM // 4
    lane = jnp.arange(2 * LANES, dtype=jnp.int32)
    in_head = lane % HEAD_DIM
    freqs = ROPE_THETA ** (-(in_head % quarter).astype(F32) / quarter)
    token = jnp.arange(n_tokens, dtype=jnp.int32)
    rows = (token // GRID_W).astype(F32)
    cols = (token % GRID_W).astype(F32)
    pos = jnp.where((in_head >= HEAD_DIM // 2)[None, :], cols[:, None], rows[:, None])
    ang = pos * freqs[None, :]
    sign = jnp.where(in_head % (2 * quarter) < quarter, -1.0, 1.0).astype(F32)
    return jnp.where((lane < LANES)[None, :], jnp.cos(ang), sign[None, :] * jnp.sin(ang))


def _identity_tables(n_tokens):
    return jnp.concatenate([jnp.ones((n_tokens, LANES), F32), jnp.zeros((n_tokens, LANES), F32)], axis=-1)


def kernel(x, c, ctx, c_ctx, w_mod, b_mod, w_in, q_gain, k_gain, w_proj_attn, w_short_conv, w_proj_short, w_conf_dw, b_conf_dw, conf_ln_g, conf_ln_b, w_proj_conf, b_proj_conf, w_out, post_ln_g, post_ln_b):
    n_batch, seq, _ = x.shape
    n_ctx = ctx.shape[1]
    lat_tile = 512
    lat_sub = 256
    ctx_tile = n_ctx
    attn_tile = 1024
    attn_rows = 256
    merge_tile = 512
    merge_sub = 256
    row_chunk = 64

    tabs_lat = _rope_tables(seq)
    tabs_ctx = _identity_tables(n_ctx)

    c_all = jnp.concatenate(
        [c, c_ctx[None, :], jnp.zeros((MOD_ROWS - n_batch - 1, D_MODEL), F32)], axis=0)
    mod_all = _modulation(c_all, w_mod, b_mod)

    h_lat, h_ctx = x, ctx
    for l in range(DEPTH):
        last = l == DEPTH - 1
        mod_lat = mod_all[l, :n_batch][:, None, :]
        mod_ctx = jnp.broadcast_to(mod_all[l, n_batch][None, None, :], (n_batch, 1, 3 * D_MODEL))
        w_bf = w_in[l].astype(BF16)
        qg = jnp.tile(q_gain[l], 2)[None, :]
        kg = jnp.tile(k_gain[l], 2)[None, :]
        branch_w = (w_proj_attn[l].astype(BF16), w_proj_short[l].astype(BF16),
                    w_proj_conf[l].astype(BF16), w_out[l].astype(BF16),
                    w_short_conv[l],
                    jnp.broadcast_to(w_conf_dw[l][:, None, :], (CONF_K, SUBLANES, CONF_W)),
                    b_conf_dw[l][None, :],
                    conf_ln_g[l][None, :], conf_ln_b[l][None, :], b_proj_conf[l][None, :],
                    post_ln_g[l][None, :], post_ln_b[l][None, :])

        if last:
            k4_c, v4_c = _project_kv(h_ctx, mod_ctx, w_bf[:, _OFF["k"][0]:_OFF["v"][1]],
                                     tabs_ctx, kg, ctx_tile)
        else:
            q_c, k4_c, v4_c, za_c, yc_c, bzc_c, gm_c = _project(
                h_ctx, mod_ctx, w_bf, tabs_ctx, qg, kg, ctx_tile, ctx_tile)

        q_l, k4_l, v4_l, za_l, yc_l, bzc_l, gm_l = _project(
            h_lat, mod_lat, w_bf, tabs_lat, qg, kg, lat_tile, lat_sub)
        gated_l = _attend(q_l, ((k4_c, v4_c), (k4_l, v4_l)), za_l, attn_tile, attn_rows)
        h_lat_new = _merge(gated_l, yc_l, bzc_l, gm_l, h_lat, mod_lat, branch_w,
                           merge_tile, merge_sub, row_chunk)

        if not last:
            gated_c = _attend(q_c, ((k4_c, v4_c),), za_c, ctx_tile, ctx_tile)
            h_ctx = _merge(gated_c, yc_c, bzc_c, gm_c, h_ctx, mod_ctx, branch_w,
                           ctx_tile, ctx_tile, row_chunk)
        h_lat = h_lat_new
    return h_lat
```

```python
import functools
import math

import jax
import jax.numpy as jnp
from jax import lax
from jax.experimental import pallas as pl
from jax.experimental.pallas import tpu as pltpu

D_MODEL = 1024
DEPTH = 2
GRID_W = 64
N_HEADS = 8
N_KV_HEADS = 2
HEAD_DIM = 64
GROUP = N_HEADS // N_KV_HEADS
ATTN_W = N_HEADS * HEAD_DIM
KV_W = N_KV_HEADS * HEAD_DIM
SHORT_W = 512
SHORT_K = 3
CONF_W = 512
CONF_K = 31
N_BRANCH = 3
ROPE_THETA = 10000.0
RMS_EPS = 1e-6
LN_EPS = 1e-5
D_IN = 2 * ATTN_W + 2 * KV_W + 4 * SHORT_W + 3 * CONF_W + N_BRANCH * D_MODEL

_OFF = {}
_o = 0
for _name, _w in (("q", ATTN_W), ("k", KV_W), ("v", KV_W), ("z_a", ATTN_W), ("s_b", SHORT_W),
                  ("s_c", SHORT_W), ("s_h", SHORT_W), ("z_s", SHORT_W), ("c_a", CONF_W),
                  ("c_g", CONF_W), ("z_c", CONF_W), ("g_m", N_BRANCH * D_MODEL)):
    _OFF[_name] = (_o, _o + _w)
    _o += _w

LANES = 128
SUBLANES = 8
HALO = 16
MOD_ROWS = 24
N_SLOTS = 2 * N_KV_HEADS
V_SLOT_W = 2 * LANES
VMEM_LIMIT_PROJECT = 56 * 1024 * 1024
VMEM_LIMIT_DEFAULT = 48 * 1024 * 1024
Q_SCALE = (HEAD_DIM ** -0.5) * math.log2(math.e)

F32 = jnp.float32
BF16 = jnp.bfloat16


def _sigmoid(x):
    return 1.0 / (1.0 + jnp.exp(-x))


def _silu(x):
    return x * _sigmoid(x)


def _mod_kernel(c_ref, w_ref, b_ref, o_ref):
    a = _silu(c_ref[...])
    o_ref[0] = jnp.dot(a, w_ref[0], precision=lax.Precision.HIGHEST,
                       preferred_element_type=F32) + b_ref[0]


def _modulation(c_all, w_mod, b_mod):
    return pl.pallas_call(
        _mod_kernel,
        grid=(DEPTH, 3),
        in_specs=[
            pl.BlockSpec((MOD_ROWS, D_MODEL), lambda l, j: (0, 0)),
            pl.BlockSpec((1, D_MODEL, D_MODEL), lambda l, j: (l, 0, j)),
            pl.BlockSpec((1, 1, D_MODEL), lambda l, j: (l, 0, j)),
        ],
        out_specs=pl.BlockSpec((1, MOD_ROWS, D_MODEL), lambda l, j: (l, 0, j)),
        out_shape=jax.ShapeDtypeStruct((DEPTH, MOD_ROWS, 3 * D_MODEL), F32),
        compiler_params=pltpu.CompilerParams(
            dimension_semantics=("arbitrary", "arbitrary"), vmem_limit_bytes=VMEM_LIMIT_DEFAULT),
        name="modulation",
    )(c_all, w_mod, b_mod.reshape(DEPTH, 1, 3 * D_MODEL))


def _norm_rope(x, gain, cos, sin):
    lane = lax.broadcasted_iota(jnp.int32, x.shape, 1)
    first = lane < HEAD_DIM
    sq = x * x
    s_lo = jnp.sum(jnp.where(first, sq, 0.0), axis=-1, keepdims=True)
    s_hi = jnp.sum(jnp.where(first, 0.0, sq), axis=-1, keepdims=True)
    ms = jnp.where(first, s_lo, s_hi) * (1.0 / HEAD_DIM)
    xn = x * lax.rsqrt(ms + RMS_EPS) * gain
    quarter = HEAD_DIM // 4
    up = pltpu.roll(xn, LANES - quarter, 1)
    down = pltpu.roll(xn, quarter, 1)
    partner = jnp.where((lane & (2 * quarter - 1)) < quarter, up, down)
    return xn * cos + partner * sin


def _modulate(h_ref, mod_ref, rows):
    shift = mod_ref[0, :, 0:D_MODEL]
    scale = mod_ref[0, :, D_MODEL:2 * D_MODEL]
    return (h_ref[0, rows, :] * (1.0 + scale) + shift).astype(BF16)


def _store_kv_slots(k, v, k4_ref, v4_ref, rows):
    first = lax.broadcasted_iota(jnp.int32, k.shape, 1) < HEAD_DIM
    ones = jnp.ones(k.shape, BF16)
    for t, ref in ((k, k4_ref), (v, v4_ref)):
        swapped = pltpu.roll(t, HEAD_DIM, 1)
        slots = (jnp.where(first, t, 0.0), jnp.where(first, 0.0, swapped),
                 jnp.where(first, swapped, 0.0), jnp.where(first, 0.0, t))
        for n, slot in enumerate(slots):
            ref[0, n, rows, 0:LANES] = slot.astype(BF16)
    for n in range(N_SLOTS):
        v4_ref[0, n, rows, LANES:V_SLOT_W] = ones


def _project_kernel(tile, sub, h_ref, mod_ref, w_ref, tab_ref, qg_ref, kg_ref,
                    q_ref, k4_ref, v4_ref, za_ref, yc_ref, bzc_ref):
    for s0 in range(0, tile, sub):
        rows = slice(s0, s0 + sub)
        u = _modulate(h_ref, mod_ref, rows)
        cos = tab_ref[rows, 0:LANES]
        sin = tab_ref[rows, LANES:2 * LANES]

        def proj(lo, hi, u=u):
            return jnp.dot(u, w_ref[:, lo:hi], preferred_element_type=F32)

        def named(name, proj=proj):
            return proj(*_OFF[name])

        x = named("q")
        for j in range(ATTN_W // LANES):
            q = _norm_rope(x[:, j * LANES:(j + 1) * LANES], qg_ref[...], cos, sin) * Q_SCALE
            q_ref[0, rows, j * LANES:(j + 1) * LANES] = q.astype(BF16)
        kv = proj(_OFF["k"][0], _OFF["v"][1])
        k = _norm_rope(kv[:, 0:KV_W], kg_ref[...], cos, sin)
        _store_kv_slots(k, kv[:, KV_W:2 * KV_W], k4_ref, v4_ref, rows)
        za_ref[0, rows, :] = _silu(named("z_a")).astype(BF16)
        yc_ref[0, rows, 0:SHORT_W] = (named("s_c") * named("s_h")).astype(BF16)
        yc_ref[0, rows, SHORT_W:SHORT_W + CONF_W] = (
            named("c_a") * _sigmoid(named("c_g"))).astype(BF16)
        bzc_ref[0, rows, 0:SHORT_W] = (named("s_b") * _silu(named("z_s"))).astype(BF16)
        bzc_ref[0, rows, SHORT_W:SHORT_W + CONF_W] = _silu(named("z_c")).astype(BF16)


def _project_kv_kernel(h_ref, mod_ref, w_ref, tab_ref, kg_ref, k4_ref, v4_ref):
    rows = slice(None)
    u = _modulate(h_ref, mod_ref, rows)
    cos = tab_ref[:, 0:LANES]
    sin = tab_ref[:, LANES:2 * LANES]
    kv = jnp.dot(u, w_ref[...], preferred_element_type=F32)
    k = _norm_rope(kv[:, 0:KV_W], kg_ref[...], cos, sin)
    _store_kv_slots(k, kv[:, KV_W:2 * KV_W], k4_ref, v4_ref, rows)


def _slot_spec(tm, width):
    return pl.BlockSpec((1, N_SLOTS, tm, width), lambda b, i: (b, 0, i, 0))


def _project(h, mod, w_bf, tabs, q_gain, k_gain, tm, sub):
    n_batch, seq, _ = h.shape
    tok = lambda cols: pl.BlockSpec((1, tm, cols), lambda b, i: (b, i, 0))
    flat = lambda cols: jax.ShapeDtypeStruct((n_batch, seq, cols), BF16)
    slots = lambda width: jax.ShapeDtypeStruct((n_batch, N_SLOTS, seq, width), BF16)
    return pl.pallas_call(
        functools.partial(_project_kernel, tm, sub),
        grid=(n_batch, seq // tm),
        in_specs=[
            tok(D_MODEL),
            pl.BlockSpec((1, 1, 3 * D_MODEL), lambda b, i: (b, 0, 0)),
            pl.BlockSpec((D_MODEL, D_IN), lambda b, i: (0, 0), pipeline_mode=pl.Buffered(1)),
            pl.BlockSpec((tm, 2 * LANES), lambda b, i: (i, 0)),
            pl.BlockSpec((1, LANES), lambda b, i: (0, 0)),
            pl.BlockSpec((1, LANES), lambda b, i: (0, 0)),
        ],
        out_specs=[tok(ATTN_W), _slot_spec(tm, LANES), _slot_spec(tm, V_SLOT_W), tok(ATTN_W),
                   tok(SHORT_W + CONF_W), tok(SHORT_W + CONF_W)],
        out_shape=[flat(ATTN_W), slots(LANES), slots(V_SLOT_W), flat(ATTN_W),
                   flat(SHORT_W + CONF_W), flat(SHORT_W + CONF_W)],
        compiler_params=pltpu.CompilerParams(
            dimension_semantics=("arbitrary", "arbitrary"), vmem_limit_bytes=VMEM_LIMIT_PROJECT),
        name="project",
    )(h, mod, w_bf, tabs, q_gain, k_gain)


def _project_kv(h, mod, w_kv_bf, tabs, k_gain, tm):
    n_batch, seq, _ = h.shape
    slots = lambda width: jax.ShapeDtypeStruct((n_batch, N_SLOTS, seq, width), BF16)
    return pl.pallas_call(
        _project_kv_kernel,
        grid=(n_batch, seq // tm),
        in_specs=[
            pl.BlockSpec((1, tm, D_MODEL), lambda b, i: (b, i, 0)),
            pl.BlockSpec((1, 1, 3 * D_MODEL), lambda b, i: (b, 0, 0)),
            pl.BlockSpec((D_MODEL, 2 * KV_W), lambda b, i: (0, 0)),
            pl.BlockSpec((tm, 2 * LANES), lambda b, i: (i, 0)),
            pl.BlockSpec((1, LANES), lambda b, i: (0, 0)),
        ],
        out_specs=[_slot_spec(tm, LANES), _slot_spec(tm, V_SLOT_W)],
        out_shape=[slots(LANES), slots(V_SLOT_W)],
        compiler_params=pltpu.CompilerParams(
            dimension_semantics=("arbitrary", "arbitrary"), vmem_limit_bytes=VMEM_LIMIT_DEFAULT),
        name="project_kv",
    )(h, mod, w_kv_bf, tabs, k_gain)


def _attend_kernel(n_sets, tq, rows, q_ref, *refs):
    k_refs = refs[0:2 * n_sets:2]
    v_refs = refs[1:2 * n_sets:2]
    za_ref, o_ref = refs[2 * n_sets:]
    for r0 in range(0, tq, rows):
        for j in range(ATTN_W // LANES):
            kv = (2 * j) // GROUP
            qp = q_ref[0, r0:r0 + rows, j * LANES:(j + 1) * LANES]
            acc = None
            for half in range(2):
                slot = 2 * kv + half
                scores = [lax.dot_general(qp, k_ref[0, slot], (((1,), (1,)), ((), ())),
                                          preferred_element_type=F32) for k_ref in k_refs]
                m = None
                for s in scores:
                    ms = jnp.max(s, axis=-1, keepdims=True)
                    m = ms if m is None else jnp.maximum(m, ms)
                o = None
                for s, v_ref in zip(scores, v_refs):
                    part = jnp.dot(jnp.exp2(s - m).astype(BF16), v_ref[0, slot],
                                   preferred_element_type=F32)
                    o = part if o is None else o + part
                o = o[:, 0:LANES] * (1.0 / o[:, LANES:V_SLOT_W])
                acc = o if acc is None else acc + o
            za = za_ref[0, r0:r0 + rows, j * LANES:(j + 1) * LANES].astype(F32)
            o_ref[0, r0:r0 + rows, j * LANES:(j + 1) * LANES] = (acc * za).astype(BF16)


def _attend(q, kv_sets, za, tq, rows):
    n_batch, seq, _ = q.shape
    tok = pl.BlockSpec((1, tq, ATTN_W), lambda b, i: (b, i, 0))
    specs, operands = [], []
    for k4, v4 in kv_sets:
        n_keys = k4.shape[2]
        specs.append(pl.BlockSpec((1, N_SLOTS, n_keys, LANES), lambda b, i: (b, 0, 0, 0)))
        specs.append(pl.BlockSpec((1, N_SLOTS, n_keys, V_SLOT_W), lambda b, i: (b, 0, 0, 0)))
        operands += [k4, v4]
    return pl.pallas_call(
        functools.partial(_attend_kernel, len(kv_sets), tq, rows),
        grid=(n_batch, seq // tq),
        in_specs=[tok] + specs + [tok],
        out_specs=tok,
        out_shape=jax.ShapeDtypeStruct((n_batch, seq, ATTN_W), BF16),
        compiler_params=pltpu.CompilerParams(
            dimension_semantics=("arbitrary", "arbitrary"), vmem_limit_bytes=VMEM_LIMIT_DEFAULT),
        name="attend",
    )(q, *operands, za)


def _layer_norm(x, g, b):
    mu = jnp.mean(x, axis=-1, keepdims=True)
    xc = x - mu
    var = jnp.mean(xc * xc, axis=-1, keepdims=True)
    return xc * lax.rsqrt(var + LN_EPS) * g + b


def _merge_kernel(tile, sub, row_chunk, alpha,
                  ga_ref, yc_ref, ycp_ref, ycn_ref, bzc_ref, wg_ref, h_ref, mod_ref,
                  wpa_ref, wps_ref, wpc_ref, wo_ref, wsc_ref, wcdw_ref, bcdw_ref,
                  clng_ref, clnb_ref, bpc_ref, plg_ref, plb_ref,
                  o_ref, xbuf, xs, act_b, act_c):
    i = pl.program_id(1)
    n_tiles = pl.num_programs(1)
    conv_w = SHORT_W + CONF_W
    groups = row_chunk // SUBLANES

    xbuf[0:HALO] = ycp_ref[0].astype(F32)
    xbuf[HALO:HALO + tile] = yc_ref[0].astype(F32)
    xbuf[HALO + tile:HALO + tile + HALO] = ycn_ref[0].astype(F32)

    @pl.when(i == 0)
    def _():
        xbuf[0:HALO] = jnp.zeros((HALO, conv_w), F32)

    @pl.when(i == n_tiles - 1)
    def _():
        xbuf[HALO + tile:HALO + tile + HALO] = jnp.zeros((HALO, conv_w), F32)

    xs_groups = xs.shape[1]
    hc_in = xbuf[:, SHORT_W:conv_w]
    hc3 = hc_in.reshape(hc_in.shape[0] // SUBLANES, SUBLANES, CONF_W)
    sublane = lax.broadcasted_iota(jnp.int32, (1, SUBLANES, CONF_W), 1)
    xs[0] = hc3[0:xs_groups]
    rot = hc3
    for r in range(1, SUBLANES):
        rot = pltpu.roll(rot, SUBLANES - 1, 1)
        xs[r] = jnp.where(sublane < SUBLANES - r, rot[0:xs_groups], rot[1:xs_groups + 1])

    shift = mod_ref[0, :, 0:D_MODEL]
    scale = mod_ref[0, :, D_MODEL:2 * D_MODEL]
    gate = mod_ref[0, :, 2 * D_MODEL:3 * D_MODEL]
    for s0 in range(0, tile, sub):
        for r0 in range(s0, s0 + sub, row_chunk):
            acc = None
            for k in range(SHORT_K):
                start = HALO + r0 + k - SHORT_K // 2
                term = xbuf[start:start + row_chunk, 0:SHORT_W] * wsc_ref[k:k + 1, :]
                acc = term if acc is None else acc + term
            bz = bzc_ref[0, r0:r0 + row_chunk, 0:SHORT_W].astype(F32)
            act_b[r0:r0 + row_chunk] = (acc * bz).astype(BF16)

            acc = None
            for k in range(CONF_K):
                start = HALO + r0 + k - CONF_K // 2
                r = start % SUBLANES
                g0 = (start - r) // SUBLANES
                term = xs[r, g0:g0 + groups] * wcdw_ref[k]
                acc = term if acc is None else acc + term
            conv = acc.reshape(row_chunk, CONF_W) + bcdw_ref[...]
            hc = _silu(_layer_norm(conv, clng_ref[...], clnb_ref[...]))
            zc = bzc_ref[0, r0:r0 + row_chunk, SHORT_W:conv_w].astype(F32)
            act_c[r0:r0 + row_chunk] = (hc * zc).astype(BF16)

        rows = slice(s0, s0 + sub)
        br_a = jnp.dot(ga_ref[0, rows, :], wpa_ref[...], preferred_element_type=F32)
        br_b = jnp.dot(act_b[rows, :], wps_ref[...], preferred_element_type=F32)
        br_c = jnp.dot(act_c[rows, :], wpc_ref[...], preferred_element_type=F32) + bpc_ref[...]
        u = (h_ref[0, rows, :] * (1.0 + scale) + shift).astype(BF16)
        gates = [_sigmoid(jnp.dot(u, wg_ref[:, j * D_MODEL:(j + 1) * D_MODEL],
                                  preferred_element_type=F32)).astype(BF16) for j in range(N_BRANCH)]
        merged = (gates[0] * br_a.astype(BF16) + gates[1] * br_b.astype(BF16)
                  + gates[2] * br_c.astype(BF16))
        out = jnp.dot(merged, wo_ref[...], preferred_element_type=F32)
        o_ref[0, rows, :] = _layer_norm(alpha * h_ref[0, rows, :] + gate * out,
                                        plg_ref[...], plb_ref[...])


def _merge(gated, yc, bzc, w_gate, h, mod, w, tile, sub, row_chunk):
    n_batch, seq, _ = h.shape
    halo_blocks = tile // HALO
    n_halo = seq // HALO
    tok = lambda cols: pl.BlockSpec((1, tile, cols), lambda b, i: (b, i, 0))
    const = lambda shape: pl.BlockSpec(shape, lambda b, i: (0,) * len(shape))
    conv_w = SHORT_W + CONF_W
    alpha = (2.0 * DEPTH) ** 0.25
    xs_rows = tile + HALO + SUBLANES
    return pl.pallas_call(
        functools.partial(_merge_kernel, tile, sub, row_chunk, alpha),
        grid=(n_batch, seq // tile),
        in_specs=[
            tok(ATTN_W),
            tok(conv_w),
            pl.BlockSpec((1, HALO, conv_w), lambda b, i: (b, jnp.maximum(i * halo_blocks - 1, 0), 0)),
            pl.BlockSpec((1, HALO, conv_w),
                         lambda b, i: (b, jnp.minimum((i + 1) * halo_blocks, n_halo - 1), 0)),
            tok(conv_w),
            pl.BlockSpec((D_MODEL, N_BRANCH * D_MODEL), lambda b, i: (0, 0),
                         pipeline_mode=pl.Buffered(1)),
            tok(D_MODEL),
            pl.BlockSpec((1, 1, 3 * D_MODEL), lambda b, i: (b, 0, 0)),
            const((ATTN_W, D_MODEL)), const((SHORT_W, D_MODEL)), const((CONF_W, D_MODEL)),
            const((D_MODEL, D_MODEL)),
            const((SHORT_K, SHORT_W)), const((CONF_K, SUBLANES, CONF_W)), const((1, CONF_W)),
            const((1, CONF_W)), const((1, CONF_W)), const((1, D_MODEL)),
            const((1, D_MODEL)), const((1, D_MODEL)),
        ],
        out_specs=tok(D_MODEL),
        out_shape=jax.ShapeDtypeStruct((n_batch, seq, D_MODEL), F32),
        scratch_shapes=[
            pltpu.VMEM((tile + 2 * HALO, conv_w), F32),
            pltpu.VMEM((SUBLANES, xs_rows // SUBLANES, SUBLANES, CONF_W), F32),
            pltpu.VMEM((tile, SHORT_W), BF16),
            pltpu.VMEM((tile, CONF_W), BF16),
        ],
        compiler_params=pltpu.CompilerParams(
            dimension_semantics=("arbitrary", "arbitrary"), vmem_limit_bytes=VMEM_LIMIT_DEFAULT),
        name="merge",
    )(gated, yc, yc, yc, bzc, w_gate, h, mod, *w)


def _rope_tables(n_tokens):
    quarter = HEAD_DIM // 4
    lane = jnp.arange(2 * LANES, dtype=jnp.int32)
    in_head = lane % HEAD_DIM
    freqs = ROPE_THETA ** (-(in_head % quarter).astype(F32) / quarter)
    token = jnp.arange(n_tokens, dtype=jnp.int32)
    rows = (token // GRID_W).astype(F32)
    cols = (token % GRID_W).astype(F32)
    pos = jnp.where((in_head >= HEAD_DIM // 2)[None, :], cols[:, None], rows[:, None])
    ang = pos * freqs[None, :]
    sign = jnp.where(in_head % (2 * quarter) < quarter, -1.0, 1.0).astype(F32)
    return jnp.where((lane < LANES)[None, :], jnp.cos(ang), sign[None, :] * jnp.sin(ang))


def _identity_tables(n_tokens):
    return jnp.concatenate([jnp.ones((n_tokens, LANES), F32), jnp.zeros((n_tokens, LANES), F32)], axis=-1)


def kernel(x, c, ctx, c_ctx, w_mod, b_mod, w_in, q_gain, k_gain, w_proj_attn, w_short_conv, w_proj_short, w_conf_dw, b_conf_dw, conf_ln_g, conf_ln_b, w_proj_conf, b_proj_conf, w_out, post_ln_g, post_ln_b):
    n_batch, seq, _ = x.shape
    n_ctx = ctx.shape[1]
    lat_tile = 512
    lat_sub = 256
    ctx_tile = n_ctx
    attn_tile = 1024
    attn_rows = 256
    merge_tile = 512
    merge_sub = 256
    row_chunk = 64

    tabs_lat = _rope_tables(seq)
    tabs_ctx = _identity_tables(n_ctx)

    c_all = jnp.concatenate(
        [c, c_ctx[None, :], jnp.zeros((MOD_ROWS - n_batch - 1, D_MODEL), F32)], axis=0)
    mod_all = _modulation(c_all, w_mod, b_mod)

    h_lat, h_ctx = x, ctx
    for l in range(DEPTH):
        last = l == DEPTH - 1
        mod_lat = mod_all[l, :n_batch][:, None, :]
        mod_ctx = jnp.broadcast_to(mod_all[l, n_batch][None, None, :], (n_batch, 1, 3 * D_MODEL))
        w_bf = w_in[l].astype(BF16)
        w_gate = w_bf[:, _OFF["g_m"][0]:_OFF["g_m"][1]]
        qg = jnp.tile(q_gain[l], 2)[None, :]
        kg = jnp.tile(k_gain[l], 2)[None, :]
        branch_w = (w_proj_attn[l].astype(BF16), w_proj_short[l].astype(BF16),
                    w_proj_conf[l].astype(BF16), w_out[l].astype(BF16),
                    w_short_conv[l],
                    jnp.broadcast_to(w_conf_dw[l][:, None, :], (CONF_K, SUBLANES, CONF_W)),
                    b_conf_dw[l][None, :],
                    conf_ln_g[l][None, :], conf_ln_b[l][None, :], b_proj_conf[l][None, :],
                    post_ln_g[l][None, :], post_ln_b[l][None, :])

        if last:
            k4_c, v4_c = _project_kv(h_ctx, mod_ctx, w_bf[:, _OFF["k"][0]:_OFF["v"][1]],
                                     tabs_ctx, kg, ctx_tile)
        else:
            q_c, k4_c, v4_c, za_c, yc_c, bzc_c = _project(
                h_ctx, mod_ctx, w_bf, tabs_ctx, qg, kg, ctx_tile, ctx_tile)

        q_l, k4_l, v4_l, za_l, yc_l, bzc_l = _project(
            h_lat, mod_lat, w_bf, tabs_lat, qg, kg, lat_tile, lat_sub)
        gated_l = _attend(q_l, ((k4_c, v4_c), (k4_l, v4_l)), za_l, attn_tile, attn_rows)
        h_lat_new = _merge(gated_l, yc_l, bzc_l, w_gate, h_lat, mod_lat, branch_w,
                           merge_tile, merge_sub, row_chunk)

        if not last:
            gated_c = _attend(q_c, ((k4_c, v4_c),), za_c, ctx_tile, ctx_tile)
            h_ctx = _merge(gated_c, yc_c, bzc_c, w_gate, h_ctx, mod_ctx, branch_w,
                           ctx_tile, ctx_tile, row_chunk)
        h_lat = h_lat_new
    return h_lat
```
